```python
import math
import jax
import jax.numpy as jnp
from jax import lax

D_MODEL = 4096
BATCH = 1
SEQ = 16384
DEPTH = 2
DEC_BATCH = 32
DEC_SEQ = 32
PAST_LEN = 1024

CHUNK = 64
HEAD_DIM = 128
A_HEADS = D_MODEL // 256
A_BACK_CHUNKS = 8
REL_CLIP = 128
B_HEADS = D_MODEL // 256
B_KV_HEADS = max(1, B_HEADS // 8)
B_GROUP = B_HEADS // B_KV_HEADS
B_WINDOW = 128
B_BACK_CHUNKS = B_WINDOW // CHUNK
A_WIDTH = A_HEADS * HEAD_DIM
B_WIDTH = B_HEADS * HEAD_DIM
B_KV_WIDTH = B_KV_HEADS * HEAD_DIM
EVEN_IN = 4 * A_WIDTH + 2 * B_WIDTH + 2 * B_KV_WIDTH
EVEN_MIX = A_WIDTH + B_WIDTH
C_HEAD_DIM = 64
C_WIDTH = D_MODEL
C_HEADS = C_WIDTH // C_HEAD_DIM
C_GROUPS = 8
C_STATE = 128
C_CONV_W = 4
C_CONV_DIM = C_WIDTH + 2 * C_GROUPS * C_STATE
SSD_CHUNK = CHUNK
D_WIDTH = D_MODEL // 2
D_CONV_W = 3
ODD_IN = C_WIDTH + C_CONV_DIM + C_HEADS + 4 * D_WIDTH
ODD_MIX = C_WIDTH + D_WIDTH
RMS_EPS = 1e-6
NEG_INF = -1e30

kernel_name = 'chunk_causal_hybrid_stream_encoder_step'


def _split_cols(p, sizes):
    out, s = [], 0
    for n in sizes:
        out.append(p[..., s:s + n])
        s += n
    return out


def _rmsnorm(x, g):
    xf = x.astype(jnp.float32)
    y = xf * lax.rsqrt(jnp.mean(xf * xf, axis=-1, keepdims=True) + RMS_EPS)
    return (y * g.astype(jnp.float32)).astype(x.dtype)


def _alibi_slopes(n):
    return 2.0 ** (-8.0 * jnp.arange(1, n + 1, dtype=jnp.float32) / n)


def _attend(q, k, v, q_pos, k_pos, n_back, bias, sinks):
    logits = jnp.einsum('bqkgd,bskd->bkgqs', q, k).astype(jnp.float32) * (HEAD_DIM ** -0.5) + bias
    qc = (q_pos // CHUNK)[:, None]
    kc = (k_pos // CHUNK)[None, :]
    allowed = (k_pos[None, :] >= 0) & (kc <= qc) & (kc >= qc - n_back)
    logits = jnp.where(allowed, logits, NEG_INF)
    if sinks is None:
        probs = jax.nn.softmax(logits, axis=-1)
    else:
        sink_col = jnp.broadcast_to(sinks[None, :, :, None, None], logits.shape[:-1] + (1,))
        probs = jax.nn.softmax(jnp.concatenate([logits, sink_col], axis=-1), axis=-1)[..., :-1]
    return jnp.einsum('bkgqs,bskd->bqkgd', probs.astype(v.dtype), v)


def _band_attention_prompt(q, k, v, n_back, bias_fn, sinks):
    b, t = q.shape[:2]
    pad = n_back * CHUNK
    band = pad + CHUNK
    widths = [(0, 0), (pad, 0), (0, 0), (0, 0)]
    kp, vp = jnp.pad(k, widths), jnp.pad(v, widths)

    def one_chunk(c):
        start = c * CHUNK
        qi = lax.dynamic_slice_in_dim(q, start, CHUNK, axis=1)
        ki = lax.dynamic_slice_in_dim(kp, start, band, axis=1)
        vi = lax.dynamic_slice_in_dim(vp, start, band, axis=1)
        q_pos = start + jnp.arange(CHUNK, dtype=jnp.int32)
        k_pos = start - pad + jnp.arange(band, dtype=jnp.int32)
        return _attend(qi, ki, vi, q_pos, k_pos, n_back, bias_fn(q_pos[:, None] - k_pos[None, :]), sinks)

    out = lax.map(one_chunk, jnp.arange(t // CHUNK, dtype=jnp.int32))
    return jnp.moveaxis(out, 0, 1).reshape((b, t) + out.shape[3:])


def _band_attention_step(q, k, v, cache_k, cache_v, n_back, bias_fn, sinks):
    t, c = q.shape[1], cache_k.shape[1]
    k_all = jnp.concatenate([cache_k.astype(k.dtype), k], axis=1)
    v_all = jnp.concatenate([cache_v.astype(v.dtype), v], axis=1)
    q_pos = PAST_LEN + jnp.arange(t, dtype=jnp.int32)
    k_pos = PAST_LEN - c + jnp.arange(c + t, dtype=jnp.int32)
    return _attend(q, k_all, v_all, q_pos, k_pos, n_back, bias_fn(q_pos[:, None] - k_pos[None, :]), sinks)


def _attn_layer(x, norm_g, w_in, w_out, rel_table, sinks, caches):
    b, t, _ = x.shape
    h = _rmsnorm(x, norm_g)
    p = h @ w_in
    qa, ka, va, ga, qb, gb, kb, vb = _split_cols(p, [A_WIDTH] * 4 + [B_WIDTH] * 2 + [B_KV_WIDTH] * 2)
    qa = qa.reshape(b, t, A_HEADS, 1, HEAD_DIM)
    ka = ka.reshape(b, t, A_HEADS, HEAD_DIM)
    va = va.reshape(b, t, A_HEADS, HEAD_DIM)
    qb = qb.reshape(b, t, B_KV_HEADS, B_GROUP, HEAD_DIM)
    kb = kb.reshape(b, t, B_KV_HEADS, HEAD_DIM)
    vb = vb.reshape(b, t, B_KV_HEADS, HEAD_DIM)
    table = rel_table.astype(jnp.float32)
    slopes = _alibi_slopes(B_HEADS)
    bias_a = lambda d: table[:, jnp.clip(d, -REL_CLIP, REL_CLIP) + REL_CLIP][:, None]
    bias_b = lambda d: (-slopes[:, None, None] * jnp.abs(d).astype(jnp.float32)).reshape((B_KV_HEADS, B_GROUP) + d.shape)
    sinks_b = sinks.astype(jnp.float32).reshape(B_KV_HEADS, B_GROUP)
    if caches is None:
        oa = _band_attention_prompt(qa, ka, va, A_BACK_CHUNKS, bias_a, None)
        ob = _band_attention_prompt(qb, kb, vb, B_BACK_CHUNKS, bias_b, sinks_b)
        keep_a = min(A_BACK_CHUNKS * CHUNK, t)
        keep_b = min(B_WINDOW, t)
        new = (ka[:, t - keep_a:], va[:, t - keep_a:], kb[:, t - keep_b:], vb[:, t - keep_b:])
    else:
        ck_a, cv_a, ck_b, cv_b = caches
        oa = _band_attention_step(qa, ka, va, ck_a, cv_a, A_BACK_CHUNKS, bias_a, None)
        ob = _band_attention_step(qb, kb, vb, ck_b, cv_b, B_BACK_CHUNKS, bias_b, sinks_b)
        new = (ka, va, kb, vb)
    mixed = jnp.concatenate([oa.reshape(b, t, A_WIDTH) * jax.nn.silu(ga),
                             ob.reshape(b, t, B_WIDTH) * jax.nn.silu(gb)], axis=-1)
    return x + mixed @ w_out, new


def _causal_dwconv(u, buf, w):
    k = w.shape[0]
    full = jnp.concatenate([buf.astype(u.dtype), u], axis=1)
    out = lax.conv_general_dilated(full, w[:, None, :].astype(u.dtype), (1,), 'VALID',
                                   dimension_numbers=('NWC', 'WIO', 'NWC'),
                                   feature_group_count=u.shape[-1])
    return out, full[:, full.shape[1] - (k - 1):]


def _ssd_scan(x, dt, a, bm, cm, h0):
    b, l = x.shape[:2]
    hpg = C_HEADS // C_GROUPS
    pad = (-l) % SSD_CHUNK
    padt = lambda u: jnp.pad(u, [(0, 0), (0, pad)] + [(0, 0)] * (u.ndim - 2))
    x, dt, bm, cm = padt(x), padt(dt), padt(bm), padt(cm)
    nc = (l + pad) // SSD_CHUNK
    to_chunks = lambda u, tail: jnp.moveaxis(u.reshape((b, nc, SSD_CHUNK) + tail), 1, 0)
    xs = to_chunks(x, (C_GROUPS, hpg, C_HEAD_DIM))
    dts = to_chunks(dt, (C_GROUPS, hpg))
    bs = to_chunks(bm, (C_GROUPS, C_STATE))
    cs = to_chunks(cm, (C_GROUPS, C_STATE))
    a_g = a.reshape(C_GROUPS, hpg)
    causal = jnp.tril(jnp.ones((SSD_CHUNK, SSD_CHUNK), dtype=bool))[:, :, None, None]

    def step(h, inp):
        xc, dtc, bc, cc = inp
        cum = jnp.cumsum(dtc * a_g, axis=1)
        seg = cum[:, :, None] - cum[:, None]
        lmat = jnp.where(causal, jnp.exp(jnp.where(causal, seg, 0.0)), 0.0)
        xdt = xc * dtc[..., None]
        scores = jnp.einsum('blgn,bsgn->blsg', cc, bc)
        y_diag = jnp.einsum('blsg,blsgh,bsghp->blghp', scores, lmat, xdt)
        y_off = jnp.einsum('blgn,bghpn,blgh->blghp', cc, h, jnp.exp(cum))
        decay_to_end = jnp.exp(cum[:, -1:] - cum)
        h_new = jnp.exp(cum[:, -1])[..., None, None] * h + jnp.einsum('bsgn,bsgh,bsghp->bghpn', bc, decay_to_end, xdt)
        return h_new, y_diag + y_off

    h_final, ys = lax.scan(step, h0.reshape(b, C_GROUPS, hpg, C_HEAD_DIM, C_STATE), (xs, dts, bs, cs))
    y = jnp.moveaxis(ys, 0, 1).reshape(b, nc * SSD_CHUNK, C_HEADS, C_HEAD_DIM)[:, :l]
    return y, h_final.reshape(b, C_HEADS, C_HEAD_DIM, C_STATE)


def _ssm_conv_layer(x, norm_g, w_in, w_out, conv_c_w, conv_c_b, dt_bias, a_log, d_skip, c_norm_g, conv_d_w,
                    ssm_state, conv_c_buf, conv_d_buf):
    b, t, _ = x.shape
    f32 = jnp.float32
    h = _rmsnorm(x, norm_g)
    p = h @ w_in
    z, xbc, dt_raw, gd, bd, cd, hd = _split_cols(p, [C_WIDTH, C_CONV_DIM, C_HEADS, D_WIDTH, D_WIDTH, D_WIDTH, D_WIDTH])
    xbc_conv, new_c_buf = _causal_dwconv(xbc, conv_c_buf, conv_c_w)
    xbc_act = jax.nn.silu(xbc_conv + conv_c_b)
    xc, bm, cm = _split_cols(xbc_act, [C_WIDTH, C_GROUPS * C_STATE, C_GROUPS * C_STATE])
    dt = jax.nn.softplus(dt_raw.astype(f32) + dt_bias.astype(f32))
    a = -jnp.exp(a_log.astype(f32))
    xh = xc.reshape(b, t, C_HEADS, C_HEAD_DIM).astype(f32)
    y, new_state = _ssd_scan(xh, dt, a, bm.reshape(b, t, C_GROUPS, C_STATE).astype(f32),
                             cm.reshape(b, t, C_GROUPS, C_STATE).astype(f32), ssm_state.astype(f32))
    y = y + d_skip.astype(f32)[:, None] * xh
    yc = _rmsnorm(y.reshape(b, t, C_WIDTH) * jax.nn.silu(z.astype(f32)), c_norm_g).astype(x.dtype)
    v, new_d_buf = _causal_dwconv(cd * hd, conv_d_buf, conv_d_w)
    yd = bd * v * jax.nn.silu(gd)
    out = jnp.concatenate([yc, yd], axis=-1) @ w_out
    return x + out, (new_state.astype(x.dtype), new_c_buf, new_d_buf)


def setup_inputs(seed: int = 0) -> dict:
    key = jax.random.key(seed)
    ks = jax.random.split(key, 26)
    f32 = jnp.float32
    n_even = (DEPTH + 1) // 2
    n_odd = DEPTH // 2
    a_keep = min(A_BACK_CHUNKS * CHUNK, PAST_LEN)
    b_keep = min(B_WINDOW, PAST_LEN)
    nrm = lambda k, shape, scale: jax.random.normal(k, shape, f32) * scale
    dt0 = jnp.exp(jax.random.uniform(ks[19], (n_odd, C_HEADS), f32, math.log(1e-3), math.log(1e-1)))
    return {
        'x_prompt': nrm(ks[0], (BATCH, SEQ, D_MODEL), 1.0),
        'x_sample': nrm(ks[1], (DEC_BATCH, DEC_SEQ, D_MODEL), 1.0),
        'cache_a_k': nrm(ks[2], (n_even, DEC_BATCH, a_keep, A_HEADS, HEAD_DIM), 1.0),
        'cache_a_v': nrm(ks[3], (n_even, DEC_BATCH, a_keep, A_HEADS, HEAD_DIM), 1.0),
        'cache_b_k': nrm(ks[4], (n_even, DEC_BATCH, b_keep, B_KV_HEADS, HEAD_DIM), 1.0),
        'cache_b_v': nrm(ks[5], (n_even, DEC_BATCH, b_keep, B_KV_HEADS, HEAD_DIM), 1.0),
        'state_c_ssm': nrm(ks[6], (n_odd, DEC_BATCH, C_HEADS, C_HEAD_DIM, C_STATE), 0.5),
        'state_c_conv': nrm(ks[7], (n_odd, DEC_BATCH, C_CONV_W - 1, C_CONV_DIM), 1.0),
        'state_d_conv': nrm(ks[8], (n_odd, DEC_BATCH, D_CONV_W - 1, D_WIDTH), 1.0),
        'norm_even': 1.0 + nrm(ks[9], (n_even, D_MODEL), 0.02),
        'w_in_even': nrm(ks[10], (n_even, D_MODEL, EVEN_IN), D_MODEL ** -0.5),
        'w_out_even': nrm(ks[11], (n_even, EVEN_MIX, D_MODEL), EVEN_MIX ** -0.5),
        'rel_bias_a': nrm(ks[12], (n_even, A_HEADS, 2 * REL_CLIP + 1), 0.5),
        'sinks_b': nrm(ks[13], (n_even, B_HEADS), 0.5),
        'norm_odd': 1.0 + nrm(ks[14], (n_odd, D_MODEL), 0.02),
        'w_in_odd': nrm(ks[15], (n_odd, D_MODEL, ODD_IN), D_MODEL ** -0.5),
        'w_out_odd': nrm(ks[16], (n_odd, ODD_MIX, D_MODEL), ODD_MIX ** -0.5),
        'conv_c_w': nrm(ks[17], (n_odd, C_CONV_W, C_CONV_DIM), C_CONV_W ** -0.5),
        'conv_c_b': nrm(ks[18], (n_odd, C_CONV_DIM), 0.01),
        'dt_bias': dt0 + jnp.log(-jnp.expm1(-dt0)),
        'a_log': jnp.log(jax.random.uniform(ks[20], (n_odd, C_HEADS), f32, 1.0, 16.0)),
        'd_skip': 1.0 + nrm(ks[21], (n_odd, C_HEADS), 0.1),
        'c_norm': 1.0 + nrm(ks[22], (n_odd, C_WIDTH), 0.02),
        'conv_d_w': nrm(ks[23], (n_odd, D_CONV_W, D_WIDTH), D_CONV_W ** -0.5),
        'final_norm': 1.0 + nrm(ks[24], (D_MODEL,), 0.02),
    }


def reference(x_prompt, x_sample, cache_a_k, cache_a_v, cache_b_k, cache_b_v, state_c_ssm, state_c_conv,
              state_d_conv, norm_even, w_in_even, w_out_even, rel_bias_a, sinks_b, norm_odd, w_in_odd, w_out_odd,
              conv_c_w, conv_c_b, dt_bias, a_log, d_skip, c_norm, conv_d_w, final_norm):
    xp, xs = x_prompt, x_sample
    bp = xp.shape[0]
    p_ak, p_av, p_bk, p_bv, p_ssm, p_cc, p_dc = [], [], [], [], [], [], []
    s_ak, s_av, s_bk, s_bv, s_ssm, s_cc, s_dc = [], [], [], [], [], [], []
    for layer in range(DEPTH):
        i = layer // 2
        if layer % 2 == 0:
            xp, (ak, av, bk, bv) = _attn_layer(xp, norm_even[i], w_in_even[i], w_out_even[i], rel_bias_a[i],
                                               sinks_b[i], None)
            xs, (sak, sav, sbk, sbv) = _attn_layer(xs, norm_even[i], w_in_even[i], w_out_even[i], rel_bias_a[i],
                                                   sinks_b[i], (cache_a_k[i], cache_a_v[i], cache_b_k[i], cache_b_v[i]))
            p_ak.append(ak); p_av.append(av); p_bk.append(bk); p_bv.append(bv)
            s_ak.append(sak); s_av.append(sav); s_bk.append(sbk); s_bv.append(sbv)
        else:
            zero_ssm = jnp.zeros((bp, C_HEADS, C_HEAD_DIM, C_STATE), xp.dtype)
            zero_cc = jnp.zeros((bp, C_CONV_W - 1, C_CONV_DIM), xp.dtype)
            zero_dc = jnp.zeros((bp, D_CONV_W - 1, D_WIDTH), xp.dtype)
            xp, (ssm, cc, dc) = _ssm_conv_layer(xp, norm_odd[i], w_in_odd[i], w_out_odd[i], conv_c_w[i], conv_c_b[i],
                                                dt_bias[i], a_log[i], d_skip[i], c_norm[i], conv_d_w[i],
                                                zero_ssm, zero_cc, zero_dc)
            xs, (sssm, scc, sdc) = _ssm_conv_layer(xs, norm_odd[i], w_in_odd[i], w_out_odd[i], conv_c_w[i],
                                                   conv_c_b[i], dt_bias[i], a_log[i], d_skip[i], c_norm[i],
                                                   conv_d_w[i], state_c_ssm[i], state_c_conv[i], state_d_conv[i])
            p_ssm.append(ssm); p_cc.append(cc); p_dc.append(dc)
            s_ssm.append(sssm); s_cc.append(scc); s_dc.append(sdc)
    y_prompt = _rmsnorm(xp, final_norm)
    y_sample = _rmsnorm(xs, final_norm)
    return (y_prompt, y_sample,
            jnp.stack(p_ak), jnp.stack(p_av), jnp.stack(p_bk), jnp.stack(p_bv),
            jnp.stack(p_ssm), jnp.stack(p_cc), jnp.stack(p_dc),
            jnp.stack(s_ak), jnp.stack(s_av), jnp.stack(s_bk), jnp.stack(s_bv),
            jnp.stack(s_ssm), jnp.stack(s_cc), jnp.stack(s_dc))
```

```python
import functools

import jax
import jax.numpy as jnp
from jax import lax
from jax.experimental import pallas as pl
from jax.experimental.pallas import tpu as pltpu

F32 = jnp.float32
BF16 = jnp.bfloat16

CHUNK = 64
HEAD_DIM = 128
A_BACK_CHUNKS = 8
B_BACK_CHUNKS = 2
B_GROUP = 8
REL_CLIP = 128
C_HEAD_DIM = 64
C_GROUPS = 8
C_STATE = 128
C_CONV_W = 4
D_CONV_W = 3
RMS_EPS = 1e-6
NEG_INF = -1e30
ATTN_SCALE = HEAD_DIM ** -0.5

V7X_VMEM_BYTES = 64 * 1024 * 1024
VMEM_LIMIT_BYTES = V7X_VMEM_BYTES - 8 * 1024 * 1024
LANES = 128
SUBLANES = 8

Q_BLOCK = 256
SSD_Q = CHUNK


def _params(*sem):
    return pltpu.CompilerParams(dimension_semantics=sem, vmem_limit_bytes=VMEM_LIMIT_BYTES)


def _silu(x):
    return x / (1.0 + jnp.exp(-x))


def _split3(x):
    hi = x.astype(BF16)
    r1 = x - hi.astype(F32)
    mid = r1.astype(BF16)
    lo = (r1 - mid.astype(F32)).astype(BF16)
    return hi, mid, lo


def _rmsnorm_kernel(x_ref, g_ref, o_ref):
    x = x_ref[...]
    y = x * lax.rsqrt(jnp.mean(x * x, axis=-1, keepdims=True) + RMS_EPS)
    o_ref[...] = (y * g_ref[...]).astype(o_ref.dtype)


def _rmsnorm(x2d, gain, out_dtype, rows=256):
    m, d = x2d.shape
    rows = min(rows, m)
    return pl.pallas_call(
        _rmsnorm_kernel,
        grid=(m // rows,),
        in_specs=[pl.BlockSpec((rows, d), lambda i: (i, 0)), pl.BlockSpec((1, d), lambda i: (0, 0))],
        out_specs=pl.BlockSpec((rows, d), lambda i: (i, 0)),
        out_shape=jax.ShapeDtypeStruct((m, d), out_dtype),
        compiler_params=_params("parallel"),
        name="rmsnorm",
    )(x2d, gain.reshape(1, d).astype(F32))


def _matmul_kernel(*refs, n_lhs, has_res):
    o_ref = refs[-1]
    acc = None
    for i in range(n_lhs):
        part = jnp.dot(refs[i][...], refs[n_lhs + i][...], preferred_element_type=F32)
        acc = part if acc is None else acc + part
    if has_res:
        acc = acc + refs[2 * n_lhs][...]
    o_ref[...] = acc.astype(o_ref.dtype)


def _matmul(lhs_list, w_list, out_dtype, res=None, tm=1024, tn=1024, name="matmul"):
    m = lhs_list[0].shape[0]
    n = w_list[0].shape[1]
    tm, tn = min(tm, m), min(tn, n)
    assert m % tm == 0 and n % tn == 0, (m, n, tm, tn)
    in_specs = [pl.BlockSpec((tm, a.shape[1]), lambda i, j: (i, 0)) for a in lhs_list]
    in_specs += [pl.BlockSpec((w.shape[0], tn), lambda i, j: (0, j)) for w in w_list]
    args = list(lhs_list) + list(w_list)
    if res is not None:
        in_specs.append(pl.BlockSpec((tm, tn), lambda i, j: (i, j)))
        args.append(res)
    return pl.pallas_call(
        functools.partial(_matmul_kernel, n_lhs=len(lhs_list), has_res=res is not None),
        grid=(m // tm, n // tn),
        in_specs=in_specs,
        out_specs=pl.BlockSpec((tm, tn), lambda i, j: (i, j)),
        out_shape=jax.ShapeDtypeStruct((m, n), out_dtype),
        compiler_params=_params("parallel", "arbitrary"),
        name=name,
    )(*args)


def _band_ok(i, j, var, bq, n_back):
    qc = i >> 6
    kc = j >> 6
    kpos = var * bq - n_back * CHUNK + j
    return (kc >= qc) & (kc <= qc + n_back) & (kpos >= 0)


def _bias_a_kernel(tab_ref, o_ref, *, bq, nk, n_back):
    var = pl.program_id(1)
    for c in range(nk // LANES):
        i = lax.broadcasted_iota(jnp.int32, (bq, LANES), 0)
        j = lax.broadcasted_iota(jnp.int32, (bq, LANES), 1) + c * LANES
        idx = jnp.clip(i - j + n_back * CHUNK, -REL_CLIP, REL_CLIP) + REL_CLIP
        lane = idx & (LANES - 1)
        seg = idx >> 7
        val = jnp.zeros((bq, LANES), F32)
        for s in range(tab_ref.shape[1]):
            tab = jnp.broadcast_to(tab_ref[0, s:s + 1, :], (bq, LANES))
            val = jnp.where(seg == s, jnp.take_along_axis(tab, lane, axis=1), val)
        o_ref[0, 0, :, c * LANES:(c + 1) * LANES] = jnp.where(_band_ok(i, j, var, bq, n_back), val, NEG_INF)


def _bias_b_kernel(slope_ref, o_ref, *, bq, nk, n_back):
    h = pl.program_id(0)
    var = pl.program_id(1)
    i = lax.broadcasted_iota(jnp.int32, (bq, nk), 0)
    j = lax.broadcasted_iota(jnp.int32, (bq, nk), 1)
    dist = jnp.abs(i - j + n_back * CHUNK).astype(F32)
    val = (-slope_ref[h]) * dist
    o_ref[0, 0] = jnp.where(_band_ok(i, j, var, bq, n_back), val, NEG_INF)


def _band_geometry(n_back, bq):
    back_rows = n_back * CHUNK
    kb = min(bq, back_rows)
    assert bq % kb == 0 and back_rows % kb == 0
    offs = tuple(range(-(back_rows // kb), bq // kb))
    nvar = -(-back_rows // bq) + 1
    return kb, offs, len(offs) * kb, nvar


def _bias_tiles_a(rel_table, bq):
    heads, width = rel_table.shape
    nseg = -(-width // LANES)
    tab = jnp.pad(rel_table.astype(F32), ((0, 0), (0, nseg * LANES - width))).reshape(heads, nseg, LANES)
    _, _, nk, nvar = _band_geometry(A_BACK_CHUNKS, bq)
    return pl.pallas_call(
        functools.partial(_bias_a_kernel, bq=bq, nk=nk, n_back=A_BACK_CHUNKS),
        grid=(heads, nvar),
        in_specs=[pl.BlockSpec((1, nseg, LANES), lambda h, v: (h, 0, 0))],
        out_specs=pl.BlockSpec((1, 1, bq, nk), lambda h, v: (h, v, 0, 0)),
        out_shape=jax.ShapeDtypeStruct((heads, nvar, bq, nk), F32),
        compiler_params=_params("parallel", "parallel"),
        name="bias_a",
    )(tab)


def _bias_tiles_b(heads, bq):
    slopes = 2.0 ** (-8.0 * jnp.arange(1, heads + 1, dtype=F32) / heads)
    _, _, nk, nvar = _band_geometry(B_BACK_CHUNKS, bq)
    return pl.pallas_call(
        functools.partial(_bias_b_kernel, bq=bq, nk=nk, n_back=B_BACK_CHUNKS),
        grid=(heads, nvar),
        in_specs=[pl.BlockSpec(memory_space=pltpu.SMEM)],
        out_specs=pl.BlockSpec((1, 1, bq, nk), lambda h, v: (h, v, 0, 0)),
        out_shape=jax.ShapeDtypeStruct((heads, nvar, bq, nk), F32),
        compiler_params=_params("parallel", "parallel"),
        name="bias_b",
    )(slopes)


def _attend(q, segs, sink, gate):
    logits = []
    for k, _, bias in segs:
        s = lax.dot_general(q, k, (((1,), (1,)), ((), ())), preferred_element_type=F32)
        logits.append(s * ATTN_SCALE + bias)
    m = functools.reduce(jnp.maximum, [jnp.max(s, axis=-1, keepdims=True) for s in logits])
    if sink is not None:
        m = jnp.maximum(m, sink)
    probs = [jnp.exp(s - m) for s in logits]
    denom = functools.reduce(jnp.add, [jnp.sum(p, axis=-1, keepdims=True) for p in probs])
    if sink is not None:
        denom = denom + jnp.exp(sink - m)
    o = functools.reduce(jnp.add, [jnp.dot(p.astype(BF16), v, preferred_element_type=F32)
                                   for p, (_, v, _) in zip(probs, segs)])
    return (o / denom) * _silu(gate)


def _attn_prompt_kernel(*refs, nblk, kb, has_sink):
    sink_ref, q_ref = refs[0], refs[1]
    k_refs = refs[2:2 + nblk]
    v_refs = refs[2 + nblk:2 + 2 * nblk]
    g_ref, b_ref, o_ref = refs[2 + 2 * nblk:]
    segs = [(k_refs[i][...].astype(BF16), v_refs[i][...].astype(BF16), b_ref[0, 0, :, i * kb:(i + 1) * kb])
            for i in range(nblk)]
    sink = sink_ref[pl.program_id(0)] if has_sink else None
    o_ref[...] = _attend(q_ref[...].astype(BF16), segs, sink, g_ref[...]).astype(o_ref.dtype)


def _attn_prompt(qkv, gates, bias, sinks, *, heads, group, n_back, q_col, k_col, v_col, g_col):
    t = qkv.shape[0]
    bq = Q_BLOCK
    kb, offs, nk, nvar = _band_geometry(n_back, bq)
    assert t % bq == 0 and bias.shape == (heads, nvar, bq, nk)
    r = bq // kb
    has_sink = sinks is not None
    sink_arr = sinks.astype(F32) if has_sink else jnp.zeros((heads,), F32)

    def kv_spec(col, off):
        return pl.BlockSpec((kb, HEAD_DIM), lambda h, qi: (jnp.maximum(qi * r + off, 0), col + h // group))

    in_specs = [pl.BlockSpec(memory_space=pltpu.SMEM),
                pl.BlockSpec((bq, HEAD_DIM), lambda h, qi: (qi, q_col + h))]
    in_specs += [kv_spec(k_col, off) for off in offs]
    in_specs += [kv_spec(v_col, off) for off in offs]
    in_specs += [pl.BlockSpec((bq, HEAD_DIM), lambda h, qi: (qi, g_col + h)),
                 pl.BlockSpec((1, 1, bq, nk), lambda h, qi: (h, jnp.minimum(qi, nvar - 1), 0, 0))]
    return pl.pallas_call(
        functools.partial(_attn_prompt_kernel, nblk=len(offs), kb=kb, has_sink=has_sink),
        grid=(heads, t // bq),
        in_specs=in_specs,
        out_specs=pl.BlockSpec((bq, HEAD_DIM), lambda h, qi: (qi, h)),
        out_shape=jax.ShapeDtypeStruct((t, heads * HEAD_DIM), BF16),
        compiler_params=_params("parallel", "arbitrary"),
        name="attn_prompt",
    )(sink_arr, qkv, *([qkv] * (2 * len(offs))), gates, bias)


def _attn_step_kernel(sink_ref, qkv_ref, ck_ref, cv_ref, g_ref, b_ref, o_ref, *,
                      heads, group, cache_len, q_col, k_col, v_col, has_sink):
    t = qkv_ref.shape[0]

    def head_block(ref2d, col):
        return ref2d[:, col * HEAD_DIM:(col + 1) * HEAD_DIM]

    cached = {}
    for h in range(heads):
        kvh = h // group
        if kvh not in cached:
            cached[kvh] = (ck_ref[0, :, kvh * HEAD_DIM:(kvh + 1) * HEAD_DIM].astype(BF16),
                           cv_ref[0, :, kvh * HEAD_DIM:(kvh + 1) * HEAD_DIM].astype(BF16),
                           head_block(qkv_ref, k_col + kvh).astype(BF16),
                           head_block(qkv_ref, v_col + kvh).astype(BF16))
        ck, cv, kn, vn = cached[kvh]
        segs = [(ck, cv, b_ref[h, 0, :, 0:cache_len]), (kn, vn, b_ref[h, 0, :, cache_len:cache_len + t])]
        sink = sink_ref[h] if has_sink else None
        out = _attend(head_block(qkv_ref, q_col + h).astype(BF16), segs, sink, head_block(g_ref, h))
        o_ref[:, h * HEAD_DIM:(h + 1) * HEAD_DIM] = out.astype(o_ref.dtype)


def _attn_step(qkv, cache_k, cache_v, gates, bias, sinks, *, batch, heads, group, q_col, k_col, v_col, g_blk):
    rows = qkv.shape[0]
    t = rows // batch
    cache_len, kvw = cache_k.shape[1], cache_k.shape[2]
    nvar, bq, nk = bias.shape[1:]
    assert t <= CHUNK and cache_len + t <= nk and t <= bq
    has_sink = sinks is not None
    sink_arr = sinks.astype(F32) if has_sink else jnp.zeros((heads,), F32)
    width = heads * HEAD_DIM
    return pl.pallas_call(
        functools.partial(_attn_step_kernel, heads=heads, group=group, cache_len=cache_len,
                          q_col=q_col, k_col=k_col, v_col=v_col, has_sink=has_sink),
        grid=(batch,),
        in_specs=[pl.BlockSpec(memory_space=pltpu.SMEM),
                  pl.BlockSpec((t, qkv.shape[1]), lambda b: (b, 0)),
                  pl.BlockSpec((1, cache_len, kvw), lambda b: (b, 0, 0)),
                  pl.BlockSpec((1, cache_len, kvw), lambda b: (b, 0, 0)),
                  pl.BlockSpec((t, width), lambda b: (b, g_blk)),
                  pl.BlockSpec((heads, 1, t, nk), lambda b: (0, nvar - 1, 0, 0))],
        out_specs=pl.BlockSpec((t, width), lambda b: (b, 0)),
        out_shape=jax.ShapeDtypeStruct((rows, width), BF16),
        compiler_params=_params("parallel"),
        name="attn_step",
    )(sink_arr, qkv, cache_k, cache_v, gates, bias)


def _ssd_kernel(xbc_ref, z_ref, dtr_ref, conv0_ref, h0_ref, cw_ref, cb_ref, dtb_ref, alog_ref, dsk_ref,
                cng_ref, exp_ref, y_ref, hout_ref, convout_ref, h_s, xpad_s, act_s, gated_s, *, rows):
    c = pl.program_id(1)
    q = SSD_Q
    width = y_ref.shape[2]
    gw = width // C_GROUPS
    n_heads = width // C_HEAD_DIM
    pad0 = SUBLANES

    @pl.when(c == 0)
    def _():
        h_s[...] = h0_ref[0]
        xpad_s[0:pad0, :] = conv0_ref[0]

    xpad_s[pad0:pad0 + rows, :] = xbc_ref[0]
    strip = 4 * LANES
    for s0 in range(0, xpad_s.shape[1], strip):
        acc = cb_ref[:, s0:s0 + strip]
        for k in range(C_CONV_W):
            lo = pad0 - (C_CONV_W - 1) + k
            acc = acc + cw_ref[k:k + 1, s0:s0 + strip] * xpad_s[lo:lo + rows, s0:s0 + strip]
        act_s[0:rows, s0:s0 + strip] = _silu(acc)
    if rows < q:
        act_s[rows:q, :] = jnp.zeros((q - rows, act_s.shape[1]), F32)
    convout_ref[0] = xpad_s[rows:rows + pad0, :]
    xpad_s[0:pad0, :] = xpad_s[rows:rows + pad0, :]

    row = lax.broadcasted_iota(jnp.int32, (q, LANES), 0)
    dtr = dtr_ref[0]
    if rows < q:
        dtr = jnp.concatenate([dtr, jnp.zeros((q - rows, LANES), F32)], axis=0)
    dt = jnp.where(row < rows, jax.nn.softplus(dtr + dtb_ref[...]), 0.0)
    da = dt * (-jnp.exp(alog_ref[...]))
    tril = (lax.broadcasted_iota(jnp.int32, (q, q), 0) >= lax.broadcasted_iota(jnp.int32, (q, q), 1)).astype(BF16)
    cum = functools.reduce(jnp.add, [jnp.dot(tril, part, preferred_element_type=F32) for part in _split3(da)])
    spread = _split3(jnp.concatenate([cum, dt], axis=0))

    lane = lax.broadcasted_iota(jnp.int32, (q, LANES), 1)
    causal = row >= (lane & (C_HEAD_DIM - 1))
    low_half = lax.broadcasted_iota(jnp.int32, (q, LANES), 1) < C_HEAD_DIM
    row_g = lax.broadcasted_iota(jnp.int32, (q, gw), 0)
    diag_g = row_g == (lax.broadcasted_iota(jnp.int32, (q, gw), 1) & (C_HEAD_DIM - 1))
    ssq = jnp.zeros((q, 1), F32)

    for g in range(C_GROUPS):
        cols = slice(g * gw, (g + 1) * gw)
        ex = functools.reduce(jnp.add, [jnp.dot(part, exp_ref[:, cols], preferred_element_type=F32)
                                        for part in spread])
        cexp, dtx = ex[0:q], ex[q:2 * q]
        x_g = act_s[:, cols]
        b_g = act_s[:, width + g * C_STATE:width + (g + 1) * C_STATE].astype(BF16)
        c_g = act_s[:, width + (C_GROUPS + g) * C_STATE:width + (C_GROUPS + g + 1) * C_STATE].astype(BF16)
        xdt = x_g * dtx
        cum_row = jnp.sum(jnp.where(diag_g, cexp, 0.0), axis=0, keepdims=True)
        cum_last = cexp[q - 1:q, :]
        scores = lax.dot_general(c_g, jnp.concatenate([b_g, b_g], axis=0), (((1,), (1,)), ((), ())),
                                 preferred_element_type=F32)
        h_g = h_s[:, cols]
        y_off = jnp.dot(c_g, h_g.astype(BF16), preferred_element_type=F32) * jnp.exp(cexp)
        y_diag = []
        for j in range(gw // LANES):
            pc = slice(j * LANES, (j + 1) * LANES)
            seg = cexp[:, pc] - cum_row[:, pc]
            lmat = jnp.where(causal, jnp.exp(jnp.where(causal, seg, 0.0)), 0.0)
            xp = xdt[:, pc]
            blockdiag = jnp.concatenate([jnp.where(low_half, xp, 0.0), jnp.where(low_half, 0.0, xp)], axis=0)
            y_diag.append(jnp.dot((scores * lmat).astype(BF16), blockdiag.astype(BF16),
                                  preferred_element_type=F32))
        y = jnp.concatenate(y_diag, axis=1) + y_off + dsk_ref[:, cols] * x_g
        xw = (xdt * jnp.exp(cum_last - cexp)).astype(BF16)
        h_s[:, cols] = h_g * jnp.exp(cum_last) + lax.dot_general(
            b_g, xw, (((0,), (0,)), ((), ())), preferred_element_type=F32)
        z_g = z_ref[0, :, cols]
        if rows < q:
            z_g = jnp.concatenate([z_g, jnp.zeros((q - rows, gw), F32)], axis=0)
        gated = y * _silu(z_g)
        gated_s[:, cols] = gated
        ssq = ssq + jnp.sum(gated * gated, axis=-1, keepdims=True)

    inv = lax.rsqrt(ssq * (1.0 / width) + RMS_EPS)
    y_ref[0] = ((gated_s[...] * inv)[0:rows] * cng_ref[...]).astype(y_ref.dtype)

    @pl.when(c == pl.num_programs(1) - 1)
    def _():
        hout_ref[0] = h_s[...]


def _ssd(xbc, z, dtr, conv_state, ssm_state, conv_w, conv_b, dt_bias, a_log, d_skip, c_norm, *, rows):
    b, t, conv_dim = xbc.shape
    width = z.shape[2]
    n_heads = width // C_HEAD_DIM
    assert t % rows == 0 and rows <= SSD_Q and rows % 16 == 0 and n_heads <= LANES
    pad_lanes = lambda v: jnp.pad(v.astype(F32), (0, LANES - v.shape[0])).reshape(1, LANES)
    conv0 = jnp.pad(conv_state.astype(F32), ((0, 0), (SUBLANES - (C_CONV_W - 1), 0), (0, 0)))
    h0t = jnp.swapaxes(ssm_state.astype(F32).reshape(b, width, C_STATE), 1, 2)
    expand = (jnp.arange(LANES)[:, None] == (jnp.arange(width)[None, :] // C_HEAD_DIM)).astype(BF16)
    full = lambda shape: pl.BlockSpec(shape, lambda bi, ci: (0,) * len(shape))
    per_b = lambda shape: pl.BlockSpec(shape, lambda bi, ci: (bi,) + (0,) * (len(shape) - 1))
    blk = lambda w: pl.BlockSpec((1, rows, w), lambda bi, ci: (bi, ci, 0))
    y, h_out, conv_out = pl.pallas_call(
        functools.partial(_ssd_kernel, rows=rows),
        grid=(b, t // rows),
        in_specs=[blk(conv_dim), blk(width), blk(LANES), per_b((1, SUBLANES, conv_dim)),
                  per_b((1, C_STATE, width)), full((C_CONV_W, conv_dim)), full((1, conv_dim)),
                  full((1, LANES)), full((1, LANES)), full((1, width)), full((1, width)),
                  full((LANES, width))],
        out_specs=[blk(width), per_b((1, C_STATE, width)), per_b((1, SUBLANES, conv_dim))],
        out_shape=[jax.ShapeDtypeStruct((b, t, width), BF16),
                   jax.ShapeDtypeStruct((b, C_STATE, width), F32),
                   jax.ShapeDtypeStruct((b, SUBLANES, conv_dim), F32)],
        scratch_shapes=[pltpu.VMEM((C_STATE, width), F32),
                        pltpu.VMEM((SUBLANES + rows, conv_dim), F32),
                        pltpu.VMEM((SSD_Q, conv_dim), F32),
                        pltpu.VMEM((SSD_Q, width), F32)],
        compiler_params=_params("parallel", "arbitrary"),
        name="ssd",
    )(xbc, z, dtr, conv0, h0t, conv_w.astype(F32), conv_b.astype(F32).reshape(1, conv_dim),
      pad_lanes(dt_bias), pad_lanes(a_log), jnp.repeat(d_skip.astype(F32), C_HEAD_DIM).reshape(1, width),
      c_norm.astype(F32).reshape(1, width), expand)
    new_state = jnp.swapaxes(h_out, 1, 2).reshape(b, n_heads, C_HEAD_DIM, C_STATE)
    return y, new_state, conv_out[:, SUBLANES - (C_CONV_W - 1):, :]


def _gconv_kernel(g_ref, b_ref, c_ref, h_ref, conv0_ref, w_ref, y_ref, convout_ref, upad_s, *, rows):
    pad0 = SUBLANES

    @pl.when(pl.program_id(1) == 0)
    def _():
        upad_s[0:pad0, :] = conv0_ref[0]

    upad_s[pad0:pad0 + rows, :] = c_ref[0] * h_ref[0]
    v = None
    for k in range(D_CONV_W):
        lo = pad0 - (D_CONV_W - 1) + k
        term = w_ref[k:k + 1, :] * upad_s[lo:lo + rows, :]
        v = term if v is None else v + term
    y_ref[0] = (b_ref[0] * v * _silu(g_ref[0])).astype(y_ref.dtype)
    convout_ref[0] = upad_s[rows:rows + pad0, :]
    upad_s[0:pad0, :] = upad_s[rows:rows + pad0, :]


def _gated_conv(gbch, conv_state, conv_w, *, rows):
    b, t, w4 = gbch.shape
    width = w4 // 4
    assert t % rows == 0 and rows % 16 == 0
    conv0 = jnp.pad(conv_state.astype(F32), ((0, 0), (SUBLANES - (D_CONV_W - 1), 0), (0, 0)))
    col = lambda k: pl.BlockSpec((1, rows, width), lambda bi, ci: (bi, ci, k))
    y, conv_out = pl.pallas_call(
        functools.partial(_gconv_kernel, rows=rows),
        grid=(b, t // rows),
        in_specs=[col(0), col(1), col(2), col(3),
                  pl.BlockSpec((1, SUBLANES, width), lambda bi, ci: (bi, 0, 0)),
                  pl.BlockSpec((D_CONV_W, width), lambda bi, ci: (0, 0))],
        out_specs=[pl.BlockSpec((1, rows, width), lambda bi, ci: (bi, ci, 0)),
                   pl.BlockSpec((1, SUBLANES, width), lambda bi, ci: (bi, 0, 0))],
        out_shape=[jax.ShapeDtypeStruct((b, t, width), BF16),
                   jax.ShapeDtypeStruct((b, SUBLANES, width), F32)],
        scratch_shapes=[pltpu.VMEM((SUBLANES + rows, width), F32)],
        compiler_params=_params("parallel", "arbitrary"),
        name="gated_conv",
    )(gbch, gbch, gbch, gbch, conv0, conv_w.astype(F32))
    return y, conv_out[:, SUBLANES - (D_CONV_W - 1):, :]


def _even_weights(w_in, w_out):
    aw = w_out.shape[0] // 2
    bkv = (w_in.shape[1] - 6 * aw) // 2
    o = [0, aw, 2 * aw, 3 * aw, 4 * aw, 5 * aw, 6 * aw, 6 * aw + bkv, 6 * aw + 2 * bkv]
    wb = w_in.astype(BF16)
    return dict(
        qkv_a=wb[:, o[0]:o[3]],
        qkv_b=jnp.concatenate([wb[:, o[4]:o[5]], wb[:, o[6]:o[8]]], axis=1),
        gates=jnp.concatenate([wb[:, o[3]:o[4]], wb[:, o[5]:o[6]]], axis=1),
        out_a=w_out[:aw].astype(BF16), out_b=w_out[aw:].astype(BF16), aw=aw, bkv=bkv)


def _attn_layer(x2d, hn, w, bias_a, bias_b, sinks, caches, batch):
    aw, bkv = w["aw"], w["bkv"]
    heads = aw // HEAD_DIM
    kv_dtype = BF16 if caches is None else F32
    qkv_a = _matmul([hn], [w["qkv_a"]], kv_dtype, name="in_proj_qkv_a")
    qkv_b = _matmul([hn], [w["qkv_b"]], kv_dtype, tn=512, name="in_proj_qkv_b")
    gates = _matmul([hn], [w["gates"]], F32, name="in_proj_gates")
    kvh_b = bkv // HEAD_DIM
    if caches is None:
        oa = _attn_prompt(qkv_a, gates, bias_a, None, heads=heads, group=1, n_back=A_BACK_CHUNKS,
                          q_col=0, k_col=heads, v_col=2 * heads, g_col=0)
        ob = _attn_prompt(qkv_b, gates, bias_b, sinks, heads=heads, group=heads // kvh_b, n_back=B_BACK_CHUNKS,
                          q_col=0, k_col=heads, v_col=heads + kvh_b, g_col=heads)
    else:
        ck_a, cv_a, ck_b, cv_b = caches
        flat = lambda c: c.astype(F32).reshape(c.shape[0], c.shape[1], -1)
        oa = _attn_step(qkv_a, flat(ck_a), flat(cv_a), gates, bias_a, None, batch=batch, heads=heads, group=1,
                        q_col=0, k_col=heads, v_col=2 * heads, g_blk=0)
        ob = _attn_step(qkv_b, flat(ck_b), flat(cv_b), gates, bias_b, sinks, batch=batch, heads=heads,
                        group=heads // kvh_b, q_col=0, k_col=heads, v_col=heads + kvh_b, g_blk=1)
    x_new = _matmul([oa, ob], [w["out_a"], w["out_b"]], F32, res=x2d, tn=512, name="out_proj_even")
    return x_new, qkv_a, qkv_b


def _odd_weights(w_in, w_out, c_width, conv_dim, c_heads):
    wb = w_in.astype(BF16)
    o1 = c_width + conv_dim
    return dict(
        z=wb[:, :c_width], xbc=wb[:, c_width:o1],
        dt=jnp.pad(wb[:, o1:o1 + c_heads], ((0, 0), (0, LANES - c_heads))),
        gbch=wb[:, o1 + c_heads:],
        out_c=w_out[:c_width].astype(BF16), out_d=w_out[c_width:].astype(BF16))


def _ssm_conv_layer(x2d, hn, w, p, states, batch, rows_c, rows_d):
    c_width = p["c_norm"].shape[0]
    conv_dim = p["conv_c_w"].shape[1]
    ssm_state, conv_c_state, conv_d_state = states
    t = x2d.shape[0] // batch
    z = _matmul([hn], [w["z"]], F32, name="in_proj_z").reshape(batch, t, c_width)
    xbc = _matmul([hn], [w["xbc"]], F32, name="in_proj_xbc").reshape(batch, t, conv_dim)
    dtr = _matmul([hn], [w["dt"]], F32, name="in_proj_dt").reshape(batch, t, LANES)
    gbch = _matmul([hn], [w["gbch"]], F32, name="in_proj_gbch").reshape(batch, t, -1)
    yc, new_ssm, new_cc = _ssd(xbc, z, dtr, conv_c_state, ssm_state,
                               p["conv_c_w"], p["conv_c_b"], p["dt_bias"], p["a_log"], p["d_skip"], p["c_norm"],
                               rows=rows_c)
    yd, new_dc = _gated_conv(gbch, conv_d_state, p["conv_d_w"], rows=rows_d)
    x_new = _matmul([yc.reshape(batch * t, -1), yd.reshape(batch * t, -1)], [w["out_c"], w["out_d"]], F32,
                    res=x2d, tn=512, name="out_proj_odd")
    return x_new, new_ssm, new_cc, new_dc


def kernel(x_prompt, x_sample, cache_a_k, cache_a_v, cache_b_k, cache_b_v, state_c_ssm, state_c_conv, state_d_conv, norm_even, w_in_even, w_out_even, rel_bias_a, sinks_b, norm_odd, w_in_odd, w_out_odd, conv_c_w, conv_c_b, dt_bias, a_log, d_skip, c_norm, conv_d_w, final_norm):
    bp, tp, d = x_prompt.shape
    bs, ts, _ = x_sample.shape
    assert bp == 1, "the prompt attention kernel handles one prompt sequence"
    dt_out = x_prompt.dtype
    depth = norm_even.shape[0] + norm_odd.shape[0]
    xp = x_prompt.astype(F32).reshape(bp * tp, d)
    xs = x_sample.astype(F32).reshape(bs * ts, d)
    keep_a = min(A_BACK_CHUNKS * CHUNK, tp)
    keep_b = min(B_BACK_CHUNKS * CHUNK, tp)
    outs = {k: [] for k in ("p_ak", "p_av", "p_bk", "p_bv", "p_ssm", "p_cc", "p_dc",
                            "s_ak", "s_av", "s_bk", "s_bv", "s_ssm", "s_cc", "s_dc")}
    for layer in range(depth):
        i = layer // 2
        if layer % 2 == 0:
            w = _even_weights(w_in_even[i], w_out_even[i])
            aw, bkv = w["aw"], w["bkv"]
            heads = aw // HEAD_DIM
            bias_a = _bias_tiles_a(rel_bias_a[i], Q_BLOCK)
            bias_b = _bias_tiles_b(heads, Q_BLOCK)
            hp = _rmsnorm(xp, norm_even[i], BF16)
            hs = _rmsnorm(xs, norm_even[i], BF16)
            xp_new, _, _ = _attn_layer(xp, hp, w, bias_a, bias_b, sinks_b[i], None, bp)
            kv_a = _matmul([hp[tp - keep_a:]], [w["qkv_a"][:, aw:]], F32, name="cache_kv_a")
            kv_b = _matmul([hp[tp - keep_b:]], [w["qkv_b"][:, aw:]], F32, tn=512, name="cache_kv_b")
            outs["p_ak"].append(kv_a[:, :aw].reshape(bp, keep_a, heads, HEAD_DIM))
            outs["p_av"].append(kv_a[:, aw:].reshape(bp, keep_a, heads, HEAD_DIM))
            outs["p_bk"].append(kv_b[:, :bkv].reshape(bp, keep_b, bkv // HEAD_DIM, HEAD_DIM))
            outs["p_bv"].append(kv_b[:, bkv:].reshape(bp, keep_b, bkv // HEAD_DIM, HEAD_DIM))
            xs_new, qkv_a, qkv_b = _attn_layer(xs, hs, w, bias_a, bias_b, sinks_b[i],
                                               (cache_a_k[i], cache_a_v[i], cache_b_k[i], cache_b_v[i]), bs)
            outs["s_ak"].append(qkv_a[:, aw:2 * aw].reshape(bs, ts, heads, HEAD_DIM))
            outs["s_av"].append(qkv_a[:, 2 * aw:].reshape(bs, ts, heads, HEAD_DIM))
            outs["s_bk"].append(qkv_b[:, aw:aw + bkv].reshape(bs, ts, bkv // HEAD_DIM, HEAD_DIM))
            outs["s_bv"].append(qkv_b[:, aw + bkv:].reshape(bs, ts, bkv // HEAD_DIM, HEAD_DIM))
            xp, xs = xp_new, xs_new
        else:
            c_width = c_norm.shape[1]
            conv_dim = conv_c_w.shape[2]
            c_heads = dt_bias.shape[1]
            d_width = conv_d_w.shape[2]
            w = _odd_weights(w_in_odd[i], w_out_odd[i], c_width, conv_dim, c_heads)
            p = dict(conv_c_w=conv_c_w[i], conv_c_b=conv_c_b[i], dt_bias=dt_bias[i], a_log=a_log[i],
                     d_skip=d_skip[i], c_norm=c_norm[i], conv_d_w=conv_d_w[i])
            zero = (jnp.zeros((bp, c_heads, C_HEAD_DIM, C_STATE), F32),
                    jnp.zeros((bp, C_CONV_W - 1, conv_dim), F32),
                    jnp.zeros((bp, D_CONV_W - 1, d_width), F32))
            hp = _rmsnorm(xp, norm_odd[i], BF16)
            hs = _rmsnorm(xs, norm_odd[i], BF16)
            xp, ssm, cc, dc = _ssm_conv_layer(xp, hp, w, p, zero, bp, SSD_Q, min(512, tp))
            xs, sssm, scc, sdc = _ssm_conv_layer(xs, hs, w, p, (state_c_ssm[i], state_c_conv[i], state_d_conv[i]),
                                                 bs, ts, ts)
            outs["p_ssm"].append(ssm.astype(dt_out)); outs["p_cc"].append(cc.astype(dt_out))
            outs["p_dc"].append(dc.astype(dt_out))
            outs["s_ssm"].append(sssm.astype(dt_out)); outs["s_cc"].append(scc.astype(dt_out))
            outs["s_dc"].append(sdc.astype(dt_out))
    y_prompt = _rmsnorm(xp, final_norm, dt_out).reshape(bp, tp, d)
    y_sample = _rmsnorm(xs, final_norm, dt_out).reshape(bs, ts, d)
    st = lambda k: jnp.stack(outs[k])
    return (y_prompt, y_sample, st("p_ak"), st("p_av"), st("p_bk"), st("p_bv"), st("p_ssm"), st("p_cc"), st("p_dc"),
            st("s_ak"), st("s_av"), st("s_bk"), st("s_bv"), st("s_ssm"), st("s_cc"), st("s_dc"))
```

```python
import functools

import jax
import jax.numpy as jnp
from jax import lax
from jax.experimental import pallas as pl
from jax.experimental.pallas import tpu as pltpu

F32 = jnp.float32
BF16 = jnp.bfloat16

CHUNK = 64
HEAD_DIM = 128
A_BACK_CHUNKS = 8
B_BACK_CHUNKS = 2
B_GROUP = 8
REL_CLIP = 128
C_HEAD_DIM = 64
C_GROUPS = 8
C_STATE = 128
C_CONV_W = 4
D_CONV_W = 3
RMS_EPS = 1e-6
NEG_INF = -1e30
ATTN_SCALE = HEAD_DIM ** -0.5
LOG2E = 1.4426950408889634

V7X_VMEM_BYTES = 64 * 1024 * 1024
VMEM_LIMIT_BYTES = V7X_VMEM_BYTES - 8 * 1024 * 1024
LANES = 128
SUBLANES = 8

Q_BLOCK = 256
ATTN_HEADS_PER_STEP = 8
SSD_Q = CHUNK
SSD_ROWS_PER_STEP = 2 * SSD_Q


def _params(*sem):
    return pltpu.CompilerParams(dimension_semantics=sem, vmem_limit_bytes=VMEM_LIMIT_BYTES)


def _silu(x):
    h = 0.5 * x
    return h + h * jnp.tanh(h)


def _split3(x):
    hi = x.astype(BF16)
    r1 = x - hi.astype(F32)
    mid = r1.astype(BF16)
    lo = (r1 - mid.astype(F32)).astype(BF16)
    return hi, mid, lo


def _rmsnorm_kernel(x_ref, g_ref, o_ref):
    x = x_ref[...]
    y = x * lax.rsqrt(jnp.mean(x * x, axis=-1, keepdims=True) + RMS_EPS)
    o_ref[...] = (y * g_ref[...]).astype(o_ref.dtype)


def _rmsnorm(x2d, gain, out_dtype, rows=512):
    m, d = x2d.shape
    rows = min(rows, m)
    return pl.pallas_call(
        _rmsnorm_kernel,
        grid=(m // rows,),
        in_specs=[pl.BlockSpec((rows, d), lambda i: (i, 0)), pl.BlockSpec((1, d), lambda i: (0, 0))],
        out_specs=pl.BlockSpec((rows, d), lambda i: (i, 0)),
        out_shape=jax.ShapeDtypeStruct((m, d), out_dtype),
        compiler_params=_params("parallel"),
        name="rmsnorm",
    )(x2d, gain.reshape(1, d).astype(F32))


def _matmul_kernel(*refs, n_lhs, has_res):
    o_ref = refs[-1]
    acc = None
    for i in range(n_lhs):
        part = jnp.dot(refs[i][...], refs[n_lhs + i][...], preferred_element_type=F32)
        acc = part if acc is None else acc + part
    if has_res:
        acc = acc + refs[2 * n_lhs][...]
    o_ref[...] = acc.astype(o_ref.dtype)


def _matmul(lhs_list, w_list, out_dtype, *, n=None, wcol=None, res=None, tm=1024, tn=1024, name="matmul"):
    m = lhs_list[0].shape[0]
    n = w_list[0][0].shape[1] if n is None else n
    wcol = (lambda j: j) if wcol is None else wcol
    tm, tn = min(tm, m), min(tn, n)
    assert m % tm == 0 and n % tn == 0, (m, n, tm, tn)
    in_specs = [pl.BlockSpec((tm, a.shape[1]), lambda i, j: (i, 0)) for a in lhs_list]
    in_specs += [pl.BlockSpec((a.shape[1], tn), lambda i, j, kblk=kblk: (kblk, wcol(j)))
                 for a, (_, kblk) in zip(lhs_list, w_list)]
    args = list(lhs_list) + [w for w, _ in w_list]
    if res is not None:
        in_specs.append(pl.BlockSpec((tm, tn), lambda i, j: (i, j)))
        args.append(res)
    return pl.pallas_call(
        functools.partial(_matmul_kernel, n_lhs=len(lhs_list), has_res=res is not None),
        grid=(m // tm, n // tn),
        in_specs=in_specs,
        out_specs=pl.BlockSpec((tm, tn), lambda i, j: (i, j)),
        out_shape=jax.ShapeDtypeStruct((m, n), out_dtype),
        compiler_params=_params("parallel", "arbitrary"),
        name=name,
    )(*args)


def _band_ok(i, j, var, bq, n_back):
    qc = i >> 6
    kc = j >> 6
    kpos = var * bq - n_back * CHUNK + j
    return (kc >= qc) & (kc <= qc + n_back) & (kpos >= 0)


def _bias_a_kernel(tab_ref, o_ref, *, bq, nk, n_back):
    var = pl.program_id(1)
    for c in range(nk // LANES):
        i = lax.broadcasted_iota(jnp.int32, (bq, LANES), 0)
        j = lax.broadcasted_iota(jnp.int32, (bq, LANES), 1) + c * LANES
        idx = jnp.clip(i - j + n_back * CHUNK, -REL_CLIP, REL_CLIP) + REL_CLIP
        lane = idx & (LANES - 1)
        seg = idx >> 7
        val = jnp.zeros((bq, LANES), F32)
        for s in range(tab_ref.shape[1]):
            tab = jnp.broadcast_to(tab_ref[0, s:s + 1, :], (bq, LANES))
            val = jnp.where(seg == s, jnp.take_along_axis(tab, lane, axis=1), val)
        o_ref[0, 0, :, c * LANES:(c + 1) * LANES] = jnp.where(_band_ok(i, j, var, bq, n_back), val * LOG2E, NEG_INF)


def _bias_b_kernel(slope_ref, o_ref, *, bq, nk, n_back):
    h = pl.program_id(0)
    var = pl.program_id(1)
    i = lax.broadcasted_iota(jnp.int32, (bq, nk), 0)
    j = lax.broadcasted_iota(jnp.int32, (bq, nk), 1)
    dist = jnp.abs(i - j + n_back * CHUNK).astype(F32)
    val = (-slope_ref[h]) * dist
    o_ref[0, 0] = jnp.where(_band_ok(i, j, var, bq, n_back), val * LOG2E, NEG_INF)


def _band_geometry(n_back, bq):
    back_rows = n_back * CHUNK
    kb = min(bq, back_rows)
    assert bq % kb == 0 and back_rows % kb == 0
    offs = tuple(range(-(back_rows // kb), bq // kb))
    nvar = -(-back_rows // bq) + 1
    return kb, offs, len(offs) * kb, nvar


def _bias_tiles_a(rel_table, bq):
    heads, width = rel_table.shape
    nseg = -(-width // LANES)
    tab = jnp.pad(rel_table.astype(F32), ((0, 0), (0, nseg * LANES - width))).reshape(heads, nseg, LANES)
    _, _, nk, nvar = _band_geometry(A_BACK_CHUNKS, bq)
    return pl.pallas_call(
        functools.partial(_bias_a_kernel, bq=bq, nk=nk, n_back=A_BACK_CHUNKS),
        grid=(heads, nvar),
        in_specs=[pl.BlockSpec((1, nseg, LANES), lambda h, v: (h, 0, 0))],
        out_specs=pl.BlockSpec((1, 1, bq, nk), lambda h, v: (h, v, 0, 0)),
        out_shape=jax.ShapeDtypeStruct((heads, nvar, bq, nk), F32),
        compiler_params=_params("parallel", "parallel"),
        name="bias_a",
    )(tab)


def _bias_tiles_b(heads, bq):
    slopes = 2.0 ** (-8.0 * jnp.arange(1, heads + 1, dtype=F32) / heads)
    _, _, nk, nvar = _band_geometry(B_BACK_CHUNKS, bq)
    return pl.pallas_call(
        functools.partial(_bias_b_kernel, bq=bq, nk=nk, n_back=B_BACK_CHUNKS),
        grid=(heads, nvar),
        in_specs=[pl.BlockSpec(memory_space=pltpu.SMEM)],
        out_specs=pl.BlockSpec((1, 1, bq, nk), lambda h, v: (h, v, 0, 0)),
        out_shape=jax.ShapeDtypeStruct((heads, nvar, bq, nk), F32),
        compiler_params=_params("parallel", "parallel"),
        name="bias_b",
    )(slopes)


def _attend(q, segs, sink, gate):
    logits = []
    for k, _, bias in segs:
        s = lax.dot_general(q, k, (((1,), (1,)), ((), ())), preferred_element_type=F32)
        logits.append(s * (ATTN_SCALE * LOG2E) + bias)
    m = functools.reduce(jnp.maximum, [jnp.max(s, axis=-1, keepdims=True) for s in logits])
    if sink is not None:
        m = jnp.maximum(m, sink)
    probs = [jnp.exp2(s - m) for s in logits]
    denom = functools.reduce(jnp.add, [jnp.sum(p, axis=-1, keepdims=True) for p in probs])
    if sink is not None:
        denom = denom + jnp.exp2(sink - m)
    o = functools.reduce(jnp.add, [jnp.dot(p.astype(BF16), v, preferred_element_type=F32)
                                   for p, (_, v, _) in zip(probs, segs)])
    return (o / denom) * _silu(gate)


def _attn_prompt_kernel(*refs, nblk, kb, hps, kvps, has_sink):
    sink_ref, q_ref = refs[0], refs[1]
    k_refs = refs[2:2 + nblk]
    v_refs = refs[2 + nblk:2 + 2 * nblk]
    g_ref, b_ref, o_ref = refs[2 + 2 * nblk:]
    head0 = pl.program_id(0) * hps
    kv = {}
    for hh in range(hps):
        kvi = hh * kvps // hps
        cols = slice(kvi * HEAD_DIM, (kvi + 1) * HEAD_DIM)
        if kvi not in kv:
            kv[kvi] = [(k_refs[i][:, cols].astype(BF16), v_refs[i][:, cols].astype(BF16)) for i in range(nblk)]
        segs = [(k, v, b_ref[hh, 0, :, i * kb:(i + 1) * kb]) for i, (k, v) in enumerate(kv[kvi])]
        sink = sink_ref[head0 + hh] if has_sink else None
        hc = slice(hh * HEAD_DIM, (hh + 1) * HEAD_DIM)
        o_ref[:, hc] = _attend(q_ref[:, hc].astype(BF16), segs, sink, g_ref[:, hc]).astype(o_ref.dtype)


def _attn_prompt(qkv, gates, bias, sinks, *, heads, group, n_back, q_col, k_col, v_col, g_col):
    t = qkv.shape[0]
    bq = Q_BLOCK
    hps = ATTN_HEADS_PER_STEP
    assert hps % group == 0 and heads % hps == 0
    kvps = hps // group
    kb, offs, nk, nvar = _band_geometry(n_back, bq)
    assert t % bq == 0 and bias.shape == (heads, nvar, bq, nk)
    assert q_col % hps == 0 and g_col % hps == 0 and k_col % kvps == 0 and v_col % kvps == 0
    r = bq // kb
    has_sink = sinks is not None
    sink_arr = sinks.astype(F32) * LOG2E if has_sink else jnp.zeros((heads,), F32)

    def kv_spec(col, off):
        return pl.BlockSpec((kb, kvps * HEAD_DIM),
                            lambda hb, qi: (jnp.maximum(qi * r + off, 0), col // kvps + hb))

    in_specs = [pl.BlockSpec(memory_space=pltpu.SMEM),
                pl.BlockSpec((bq, hps * HEAD_DIM), lambda hb, qi: (qi, q_col // hps + hb))]
    in_specs += [kv_spec(k_col, off) for off in offs]
    in_specs += [kv_spec(v_col, off) for off in offs]
    in_specs += [pl.BlockSpec((bq, hps * HEAD_DIM), lambda hb, qi: (qi, g_col // hps + hb)),
                 pl.BlockSpec((hps, 1, bq, nk), lambda hb, qi: (hb, jnp.minimum(qi, nvar - 1), 0, 0))]
    return pl.pallas_call(
        functools.partial(_attn_prompt_kernel, nblk=len(offs), kb=kb, hps=hps, kvps=kvps, has_sink=has_sink),
        grid=(heads // hps, t // bq),
        in_specs=in_specs,
        out_specs=pl.BlockSpec((bq, hps * HEAD_DIM), lambda hb, qi: (qi, hb)),
        out_shape=jax.ShapeDtypeStruct((t, heads * HEAD_DIM), BF16),
        compiler_params=_params("parallel", "arbitrary"),
        name="attn_prompt",
    )(sink_arr, qkv, *([qkv] * (2 * len(offs))), gates, bias)


def _attn_step_kernel(sink_ref, qkv_ref, ck_ref, cv_ref, g_ref, b_ref, o_ref, *,
                      heads, group, cache_len, q_col, k_col, v_col, has_sink):
    t = qkv_ref.shape[0]

    def head_block(ref2d, col):
        return ref2d[:, col * HEAD_DIM:(col + 1) * HEAD_DIM]

    cached = {}
    for h in range(heads):
        kvh = h // group
        if kvh not in cached:
            cached[kvh] = (ck_ref[0, :, kvh * HEAD_DIM:(kvh + 1) * HEAD_DIM].astype(BF16),
                           cv_ref[0, :, kvh * HEAD_DIM:(kvh + 1) * HEAD_DIM].astype(BF16),
                           head_block(qkv_ref, k_col + kvh).astype(BF16),
                           head_block(qkv_ref, v_col + kvh).astype(BF16))
        ck, cv, kn, vn = cached[kvh]
        segs = [(ck, cv, b_ref[h, 0, :, 0:cache_len]), (kn, vn, b_ref[h, 0, :, cache_len:cache_len + t])]
        sink = sink_ref[h] if has_sink else None
        out = _attend(head_block(qkv_ref, q_col + h).astype(BF16), segs, sink, head_block(g_ref, h))
        o_ref[:, h * HEAD_DIM:(h + 1) * HEAD_DIM] = out.astype(o_ref.dtype)


def _attn_step(qkv, cache_k, cache_v, gates, bias, sinks, *, batch, heads, group, q_col, k_col, v_col, g_blk):
    rows = qkv.shape[0]
    t = rows // batch
    cache_len, kvw = cache_k.shape[1], cache_k.shape[2]
    nvar, bq, nk = bias.shape[1:]
    assert t <= CHUNK and cache_len + t <= nk and t <= bq
    has_sink = sinks is not None
    sink_arr = sinks.astype(F32) * LOG2E if has_sink else jnp.zeros((heads,), F32)
    width = heads * HEAD_DIM
    return pl.pallas_call(
        functools.partial(_attn_step_kernel, heads=heads, group=group, cache_len=cache_len,
                          q_col=q_col, k_col=k_col, v_col=v_col, has_sink=has_sink),
        grid=(batch,),
        in_specs=[pl.BlockSpec(memory_space=pltpu.SMEM),
                  pl.BlockSpec((t, qkv.shape[1]), lambda b: (b, 0)),
                  pl.BlockSpec((1, cache_len, kvw), lambda b: (b, 0, 0)),
                  pl.BlockSpec((1, cache_len, kvw), lambda b: (b, 0, 0)),
                  pl.BlockSpec((t, width), lambda b: (b, g_blk)),
                  pl.BlockSpec((heads, 1, t, nk), lambda b: (0, nvar - 1, 0, 0))],
        out_specs=pl.BlockSpec((t, width), lambda b: (b, 0)),
        out_shape=jax.ShapeDtypeStruct((rows, width), BF16),
        compiler_params=_params("parallel"),
        name="attn_step",
    )(sink_arr, qkv, cache_k, cache_v, gates, bias)


def _ssd_kernel(xbc_ref, z_ref, dtr_ref, conv0_ref, h0_ref, cw_ref, cb_ref, dtb_ref, alog_ref, dsk_ref,
                cng_ref, exp_ref, y_ref, hout_ref, convout_ref, h_s, xpad_s, act_s, gated_s, *, rows):
    c = pl.program_id(1)
    q = SSD_Q
    width = y_ref.shape[2]
    gw = width // C_GROUPS
    n_heads = width // C_HEAD_DIM
    pad0 = SUBLANES

    @pl.when(c == 0)
    def _():
        h_s[...] = h0_ref[0]
        xpad_s[0:pad0, :] = conv0_ref[0]

    xpad_s[pad0:pad0 + rows, :] = xbc_ref[0]
    strip = 4 * LANES
    for s0 in range(0, xpad_s.shape[1], strip):
        xs = xpad_s[:, s0:s0 + strip]
        acc = cw_ref[0:1, s0:s0 + strip] * xs
        for k in range(1, C_CONV_W):
            acc = cw_ref[k:k + 1, s0:s0 + strip] * xs + pltpu.roll(acc, 1, 0)
        act_s[0:rows, s0:s0 + strip] = _silu(acc[pad0:pad0 + rows] + cb_ref[:, s0:s0 + strip])
    if rows < q:
        act_s[rows:q, :] = jnp.zeros((q - rows, act_s.shape[1]), F32)
    convout_ref[0] = xpad_s[rows:rows + pad0, :]
    xpad_s[0:pad0, :] = xpad_s[rows:rows + pad0, :]

    row = lax.broadcasted_iota(jnp.int32, (q, LANES), 0)
    lane = lax.broadcasted_iota(jnp.int32, (q, LANES), 1)
    causal = row >= (lane & (C_HEAD_DIM - 1))
    low_half = lane < C_HEAD_DIM
    diag_g = (lax.broadcasted_iota(jnp.int32, (q, gw), 0)
              == (lax.broadcasted_iota(jnp.int32, (q, gw), 1) & (C_HEAD_DIM - 1)))
    tril = (lax.broadcasted_iota(jnp.int32, (q, q), 0) >= lax.broadcasted_iota(jnp.int32, (q, q), 1)).astype(BF16)
    neg_a = -jnp.exp(alog_ref[...])
    valid = min(rows, q)

    def pad_rows(v):
        return v if valid == q else jnp.concatenate([v, jnp.zeros((q - valid, v.shape[1]), F32)], axis=0)

    for r0 in range(0, rows, q):
        dt = jnp.where(row < valid, jax.nn.softplus(pad_rows(dtr_ref[0, r0:r0 + valid, :]) + dtb_ref[...]), 0.0)
        cum = functools.reduce(jnp.add, [jnp.dot(tril, part, preferred_element_type=F32)
                                         for part in _split3(dt * neg_a)])
        spread = _split3(jnp.concatenate([cum, dt], axis=0))
        ssq = jnp.zeros((q, 1), F32)
        for g in range(C_GROUPS):
            cols = slice(g * gw, (g + 1) * gw)
            ex = functools.reduce(jnp.add, [jnp.dot(part, exp_ref[:, cols], preferred_element_type=F32)
                                            for part in spread])
            cexp, dtx = ex[0:q], ex[q:2 * q]
            x_g = act_s[r0:r0 + q, cols]
            b_g = act_s[r0:r0 + q, width + g * C_STATE:width + (g + 1) * C_STATE].astype(BF16)
            c_g = act_s[r0:r0 + q, width + (C_GROUPS + g) * C_STATE:width + (C_GROUPS + g + 1) * C_STATE].astype(BF16)
            xdt = x_g * dtx
            cum_row = jnp.sum(jnp.where(diag_g, cexp, 0.0), axis=0, keepdims=True)
            cum_last = cexp[q - 1:q, :]
            scores = lax.dot_general(c_g, jnp.concatenate([b_g, b_g], axis=0), (((1,), (1,)), ((), ())),
                                     preferred_element_type=F32)
            h_g = h_s[:, cols]
            y_off = jnp.dot(c_g, h_g.astype(BF16), preferred_element_type=F32) * jnp.exp(cexp)
            y_diag = []
            for j in range(gw // LANES):
                pc = slice(j * LANES, (j + 1) * LANES)
                seg = cexp[:, pc] - cum_row[:, pc]
                lmat = jnp.where(causal, jnp.exp(jnp.where(causal, seg, 0.0)), 0.0)
                xp = xdt[:, pc]
                blockdiag = jnp.concatenate([jnp.where(low_half, xp, 0.0), jnp.where(low_half, 0.0, xp)], axis=0)
                y_diag.append(jnp.dot((scores * lmat).astype(BF16), blockdiag.astype(BF16),
                                      preferred_element_type=F32))
            y = jnp.concatenate(y_diag, axis=1) + y_off + dsk_ref[:, cols] * x_g
            xw = (xdt * jnp.exp(cum_last - cexp)).astype(BF16)
            h_s[:, cols] = h_g * jnp.exp(cum_last) + lax.dot_general(
                b_g, xw, (((0,), (0,)), ((), ())), preferred_element_type=F32)
            gated = y * _silu(pad_rows(z_ref[0, r0:r0 + valid, cols]))
            gated_s[:, cols] = gated
            ssq = ssq + jnp.sum(gated * gated, axis=-1, keepdims=True)
        inv = lax.rsqrt(ssq * (1.0 / width) + RMS_EPS)
        y_ref[0, r0:r0 + valid, :] = ((gated_s[...] * inv)[0:valid] * cng_ref[...]).astype(y_ref.dtype)

    @pl.when(c == pl.num_programs(1) - 1)
    def _():
        hout_ref[0] = h_s[...]


def _ssd(xbc, z, dtr, conv_state, ssm_state, conv_w, conv_b, dt_bias, a_log, d_skip, c_norm, *, rows):
    b, t, conv_dim = xbc.shape
    width = z.shape[2]
    n_heads = width // C_HEAD_DIM
    assert t % rows == 0 and rows % 16 == 0 and (rows <= SSD_Q or rows % SSD_Q == 0) and n_heads <= LANES
    pad_lanes = lambda v: jnp.pad(v.astype(F32), (0, LANES - v.shape[0])).reshape(1, LANES)
    conv0 = jnp.pad(conv_state.astype(F32), ((0, 0), (SUBLANES - (C_CONV_W - 1), 0), (0, 0)))
    h0t = jnp.swapaxes(ssm_state.astype(F32).reshape(b, width, C_STATE), 1, 2)
    expand = (jnp.arange(LANES)[:, None] == (jnp.arange(width)[None, :] // C_HEAD_DIM)).astype(BF16)
    full = lambda shape: pl.BlockSpec(shape, lambda bi, ci: (0,) * len(shape))
    per_b = lambda shape: pl.BlockSpec(shape, lambda bi, ci: (bi,) + (0,) * (len(shape) - 1))
    blk = lambda w: pl.BlockSpec((1, rows, w), lambda bi, ci: (bi, ci, 0))
    y, h_out, conv_out = pl.pallas_call(
        functools.partial(_ssd_kernel, rows=rows),
        grid=(b, t // rows),
        in_specs=[blk(conv_dim), blk(width), blk(LANES), per_b((1, SUBLANES, conv_dim)),
                  per_b((1, C_STATE, width)), full((C_CONV_W, conv_dim)), full((1, conv_dim)),
                  full((1, LANES)), full((1, LANES)), full((1, width)), full((1, width)),
                  full((LANES, width))],
        out_specs=[blk(width), per_b((1, C_STATE, width)), per_b((1, SUBLANES, conv_dim))],
        out_shape=[jax.ShapeDtypeStruct((b, t, width), BF16),
                   jax.ShapeDtypeStruct((b, C_STATE, width), F32),
                   jax.ShapeDtypeStruct((b, SUBLANES, conv_dim), F32)],
        scratch_shapes=[pltpu.VMEM((C_STATE, width), F32),
                        pltpu.VMEM((SUBLANES + rows, conv_dim), F32),
                        pltpu.VMEM((max(rows, SSD_Q), conv_dim), F32),
                        pltpu.VMEM((SSD_Q, width), F32)],
        compiler_params=_params("parallel", "arbitrary"),
        name="ssd",
    )(xbc, z, dtr, conv0, h0t, conv_w.astype(F32), conv_b.astype(F32).reshape(1, conv_dim),
      pad_lanes(dt_bias), pad_lanes(a_log), jnp.repeat(d_skip.astype(F32), C_HEAD_DIM).reshape(1, width),
      c_norm.astype(F32).reshape(1, width), expand)
    new_state = jnp.swapaxes(h_out, 1, 2).reshape(b, n_heads, C_HEAD_DIM, C_STATE)
    return y, new_state, conv_out[:, SUBLANES - (C_CONV_W - 1):, :]


def _gconv_kernel(g_ref, b_ref, c_ref, h_ref, conv0_ref, w_ref, y_ref, convout_ref, upad_s, *, rows):
    pad0 = SUBLANES

    @pl.when(pl.program_id(1) == 0)
    def _():
        upad_s[0:pad0, :] = conv0_ref[0]

    upad_s[pad0:pad0 + rows, :] = c_ref[0] * h_ref[0]
    v = None
    for k in range(D_CONV_W):
        lo = pad0 - (D_CONV_W - 1) + k
        term = w_ref[k:k + 1, :] * upad_s[lo:lo + rows, :]
        v = term if v is None else v + term
    y_ref[0] = (b_ref[0] * v * _silu(g_ref[0])).astype(y_ref.dtype)
    convout_ref[0] = upad_s[rows:rows + pad0, :]
    upad_s[0:pad0, :] = upad_s[rows:rows + pad0, :]


def _gated_conv(gbch, conv_state, conv_w, *, rows):
    b, t, w4 = gbch.shape
    width = w4 // 4
    assert t % rows == 0 and rows % 16 == 0
    conv0 = jnp.pad(conv_state.astype(F32), ((0, 0), (SUBLANES - (D_CONV_W - 1), 0), (0, 0)))
    col = lambda k: pl.BlockSpec((1, rows, width), lambda bi, ci: (bi, ci, k))
    y, conv_out = pl.pallas_call(
        functools.partial(_gconv_kernel, rows=rows),
        grid=(b, t // rows),
        in_specs=[col(0), col(1), col(2), col(3),
                  pl.BlockSpec((1, SUBLANES, width), lambda bi, ci: (bi, 0, 0)),
                  pl.BlockSpec((D_CONV_W, width), lambda bi, ci: (0, 0))],
        out_specs=[pl.BlockSpec((1, rows, width), lambda bi, ci: (bi, ci, 0)),
                   pl.BlockSpec((1, SUBLANES, width), lambda bi, ci: (bi, 0, 0))],
        out_shape=[jax.ShapeDtypeStruct((b, t, width), BF16),
                   jax.ShapeDtypeStruct((b, SUBLANES, width), F32)],
        scratch_shapes=[pltpu.VMEM((SUBLANES + rows, width), F32)],
        compiler_params=_params("parallel", "arbitrary"),
        name="gated_conv",
    )(gbch, gbch, gbch, gbch, conv0, conv_w.astype(F32))
    return y, conv_out[:, SUBLANES - (D_CONV_W - 1):, :]


TN_WIDE = 1024
TN_NARROW = 512


def _even_weights(w_in, w_out):
    aw = w_out.shape[0] // 2
    bkv = (w_in.shape[1] - 6 * aw) // 2
    assert aw % TN_WIDE == 0 and (2 * bkv) % TN_NARROW == 0 and aw % TN_NARROW == 0
    return dict(w_in=w_in.astype(BF16), w_out=w_out.astype(BF16), aw=aw, bkv=bkv)


def _in_proj_even(hn, w, kv_dtype):
    aw, wb = w["aw"], w["w_in"]
    qkv_a = _matmul([hn], [(wb, 0)], kv_dtype, n=3 * aw, tn=TN_WIDE, name="in_proj_qkv_a")
    nq = aw // TN_NARROW
    qkv_b = _matmul([hn], [(wb, 0)], kv_dtype, n=aw + 2 * w["bkv"], tn=TN_NARROW, name="in_proj_qkv_b",
                    wcol=lambda j: jnp.where(j < nq, 4 * nq + j, 6 * nq + j - nq))
    ng = aw // TN_WIDE
    gates = _matmul([hn], [(wb, 0)], F32, n=2 * aw, tn=TN_WIDE, name="in_proj_gates",
                    wcol=lambda j: jnp.where(j < ng, 3 * ng + j, 5 * ng + j - ng))
    return qkv_a, qkv_b, gates


def _attn_layer(x2d, hn, w, bias_a, bias_b, sinks, caches, batch):
    aw, bkv = w["aw"], w["bkv"]
    heads = aw // HEAD_DIM
    qkv_a, qkv_b, gates = _in_proj_even(hn, w, BF16 if caches is None else F32)
    kvh_b = bkv // HEAD_DIM
    if caches is None:
        oa = _attn_prompt(qkv_a, gates, bias_a, None, heads=heads, group=1, n_back=A_BACK_CHUNKS,
                          q_col=0, k_col=heads, v_col=2 * heads, g_col=0)
        ob = _attn_prompt(qkv_b, gates, bias_b, sinks, heads=heads, group=heads // kvh_b, n_back=B_BACK_CHUNKS,
                          q_col=0, k_col=heads, v_col=heads + kvh_b, g_col=heads)
    else:
        ck_a, cv_a, ck_b, cv_b = caches
        flat = lambda c: c.reshape(c.shape[0], c.shape[1], -1).astype(BF16)
        oa = _attn_step(qkv_a, flat(ck_a), flat(cv_a), gates, bias_a, None, batch=batch, heads=heads, group=1,
                        q_col=0, k_col=heads, v_col=2 * heads, g_blk=0)
        ob = _attn_step(qkv_b, flat(ck_b), flat(cv_b), gates, bias_b, sinks, batch=batch, heads=heads,
                        group=heads // kvh_b, q_col=0, k_col=heads, v_col=heads + kvh_b, g_blk=1)
    x_new = _matmul([oa, ob], [(w["w_out"], 0), (w["w_out"], 1)], F32, res=x2d, tn=TN_NARROW, name="out_proj_even")
    return x_new, qkv_a, qkv_b


def _odd_weights(w_in, w_out, c_width, conv_dim, c_heads):
    o1 = c_width + conv_dim
    assert c_width % TN_WIDE == 0 and conv_dim % TN_WIDE == 0
    return dict(
        zxbc=w_in[:, :o1].astype(BF16),
        dt=jnp.pad(w_in[:, o1:o1 + c_heads].astype(BF16), ((0, 0), (0, LANES - c_heads))),
        gbch=w_in[:, o1 + c_heads:].astype(BF16),
        w_out=w_out.astype(BF16))


def _ssm_conv_layer(x2d, hn, w, p, states, batch, rows_c, rows_d):
    c_width = p["c_norm"].shape[0]
    conv_dim = p["conv_c_w"].shape[1]
    d_width = p["conv_d_w"].shape[1]
    ssm_state, conv_c_state, conv_d_state = states
    t = x2d.shape[0] // batch
    nz = c_width // TN_WIDE
    z = _matmul([hn], [(w["zxbc"], 0)], F32, n=c_width, tn=TN_WIDE, name="in_proj_z").reshape(batch, t, c_width)
    xbc = _matmul([hn], [(w["zxbc"], 0)], F32, n=conv_dim, tn=TN_WIDE, wcol=lambda j: nz + j,
                  name="in_proj_xbc").reshape(batch, t, conv_dim)
    dtr = _matmul([hn], [(w["dt"], 0)], F32, name="in_proj_dt").reshape(batch, t, LANES)
    gbch = _matmul([hn], [(w["gbch"], 0)], F32, tn=TN_WIDE, name="in_proj_gbch").reshape(batch, t, -1)
    yc, new_ssm, new_cc = _ssd(xbc, z, dtr, conv_c_state, ssm_state,
                               p["conv_c_w"], p["conv_c_b"], p["dt_bias"], p["a_log"], p["d_skip"], p["c_norm"],
                               rows=rows_c)
    yd, new_dc = _gated_conv(gbch, conv_d_state, p["conv_d_w"], rows=rows_d)
    assert c_width % d_width == 0
    x_new = _matmul([yc.reshape(batch * t, c_width), yd.reshape(batch * t, d_width)],
                    [(w["w_out"], 0), (w["w_out"], c_width // d_width)], F32,
                    res=x2d, tn=TN_NARROW, name="out_proj_odd")
    return x_new, new_ssm, new_cc, new_dc


def kernel(x_prompt, x_sample, cache_a_k, cache_a_v, cache_b_k, cache_b_v, state_c_ssm, state_c_conv, state_d_conv, norm_even, w_in_even, w_out_even, rel_bias_a, sinks_b, norm_odd, w_in_odd, w_out_odd, conv_c_w, conv_c_b, dt_bias, a_log, d_skip, c_norm, conv_d_w, final_norm):
    bp, tp, d = x_prompt.shape
    bs, ts, _ = x_sample.shape
    assert bp == 1, "the prompt attention kernel handles one prompt sequence"
    dt_out = x_prompt.dtype
    depth = norm_even.shape[0] + norm_odd.shape[0]
    xp = x_prompt.astype(F32).reshape(bp * tp, d)
    xs = x_sample.astype(F32).reshape(bs * ts, d)
    keep_a = min(A_BACK_CHUNKS * CHUNK, tp)
    keep_b = min(B_BACK_CHUNKS * CHUNK, tp)
    outs = {k: [] for k in ("p_ak", "p_av", "p_bk", "p_bv", "p_ssm", "p_cc", "p_dc",
                            "s_ak", "s_av", "s_bk", "s_bv", "s_ssm", "s_cc", "s_dc")}
    for layer in range(depth):
        i = layer // 2
        if layer % 2 == 0:
            w = _even_weights(w_in_even[i], w_out_even[i])
            aw, bkv = w["aw"], w["bkv"]
            heads = aw // HEAD_DIM
            bias_a = _bias_tiles_a(rel_bias_a[i], Q_BLOCK)
            bias_b = _bias_tiles_b(heads, Q_BLOCK)
            hp = _rmsnorm(xp, norm_even[i], BF16)
            hs = _rmsnorm(xs, norm_even[i], BF16)
            xp_new, _, _ = _attn_layer(xp, hp, w, bias_a, bias_b, sinks_b[i], None, bp)
            kv_a = _matmul([hp[tp - keep_a:]], [(w["w_in"], 0)], F32, n=2 * aw, tn=TN_WIDE,
                           wcol=lambda j: aw // TN_WIDE + j, name="cache_kv_a")
            kv_b = _matmul([hp[tp - keep_b:]], [(w["w_in"], 0)], F32, n=2 * bkv, tn=TN_NARROW,
                           wcol=lambda j: 6 * aw // TN_NARROW + j, name="cache_kv_b")
            outs["p_ak"].append(kv_a[:, :aw].reshape(bp, keep_a, heads, HEAD_DIM))
            outs["p_av"].append(kv_a[:, aw:].reshape(bp, keep_a, heads, HEAD_DIM))
            outs["p_bk"].append(kv_b[:, :bkv].reshape(bp, keep_b, bkv // HEAD_DIM, HEAD_DIM))
            outs["p_bv"].append(kv_b[:, bkv:].reshape(bp, keep_b, bkv // HEAD_DIM, HEAD_DIM))
            xs_new, qkv_a, qkv_b = _attn_layer(xs, hs, w, bias_a, bias_b, sinks_b[i],
                                               (cache_a_k[i], cache_a_v[i], cache_b_k[i], cache_b_v[i]), bs)
            outs["s_ak"].append(qkv_a[:, aw:2 * aw].reshape(bs, ts, heads, HEAD_DIM))
            outs["s_av"].append(qkv_a[:, 2 * aw:].reshape(bs, ts, heads, HEAD_DIM))
            outs["s_bk"].append(qkv_b[:, aw:aw + bkv].reshape(bs, ts, bkv // HEAD_DIM, HEAD_DIM))
            outs["s_bv"].append(qkv_b[:, aw + bkv:].reshape(bs, ts, bkv // HEAD_DIM, HEAD_DIM))
            xp, xs = xp_new, xs_new
        else:
            c_width = c_norm.shape[1]
            conv_dim = conv_c_w.shape[2]
            c_heads = dt_bias.shape[1]
            d_width = conv_d_w.shape[2]
            w = _odd_weights(w_in_odd[i], w_out_odd[i], c_width, conv_dim, c_heads)
            p = dict(conv_c_w=conv_c_w[i], conv_c_b=conv_c_b[i], dt_bias=dt_bias[i], a_log=a_log[i],
                     d_skip=d_skip[i], c_norm=c_norm[i], conv_d_w=conv_d_w[i])
            zero = (jnp.zeros((bp, c_heads, C_HEAD_DIM, C_STATE), F32),
                    jnp.zeros((bp, C_CONV_W - 1, conv_dim), F32),
                    jnp.zeros((bp, D_CONV_W - 1, d_width), F32))
            hp = _rmsnorm(xp, norm_odd[i], BF16)
            hs = _rmsnorm(xs, norm_odd[i], BF16)
            xp, ssm, cc, dc = _ssm_conv_layer(xp, hp, w, p, zero, bp, min(SSD_ROWS_PER_STEP, tp), min(512, tp))
            xs, sssm, scc, sdc = _ssm_conv_layer(xs, hs, w, p, (state_c_ssm[i], state_c_conv[i], state_d_conv[i]),
                                                 bs, ts, ts)
            outs["p_ssm"].append(ssm.astype(dt_out)); outs["p_cc"].append(cc.astype(dt_out))
            outs["p_dc"].append(dc.astype(dt_out))
            outs["s_ssm"].append(sssm.astype(dt_out)); outs["s_cc"].append(scc.astype(dt_out))
            outs["s_dc"].append(sdc.astype(dt_out))
    y_prompt = _rmsnorm(xp, final_norm, dt_out).reshape(bp, tp, d)
    y_sample = _rmsnorm(xs, final_norm, dt_out).reshape(bs, ts, d)
    st = lambda k: jnp.stack(outs[k])
    return (y_prompt, y_sample, st("p_ak"), st("p_av"), st("p_bk"), st("p_bv"), st("p_ssm"), st("p_cc"), st("p_dc"),
            st("s_ak"), st("s_av"), st("s_bk"), st("s_bv"), st("s_ssm"), st("s_cc"), st("s_dc"))
```

```python
import functools

import jax
import jax.numpy as jnp
from jax import lax
from jax.experimental import pallas as pl
from jax.experimental.pallas import tpu as pltpu

F32 = jnp.float32
BF16 = jnp.bfloat16

CHUNK = 64
HEAD_DIM = 128
A_BACK_CHUNKS = 8
B_BACK_CHUNKS = 2
B_GROUP = 8
REL_CLIP = 128
C_HEAD_DIM = 64
C_GROUPS = 8
C_STATE = 128
C_CONV_W = 4
D_CONV_W = 3
RMS_EPS = 1e-6
NEG_INF = -1e30
ATTN_SCALE = HEAD_DIM ** -0.5
LOG2E = 1.4426950408889634

V7X_VMEM_BYTES = 64 * 1024 * 1024
VMEM_LIMIT_BYTES = V7X_VMEM_BYTES - 8 * 1024 * 1024
LANES = 128
SUBLANES = 8

ATTN_TILE_ROWS = 128
ATTN_ROWS_PER_STEP = 256
ATTN_HEADS_PER_STEP = 8
ATTN_HEADS_PER_UNIT = 1
SSD_Q = CHUNK
SSD_ROWS_PER_STEP = 2 * SSD_Q


def _params(*sem):
    return pltpu.CompilerParams(dimension_semantics=sem, vmem_limit_bytes=VMEM_LIMIT_BYTES)


def _silu(x):
    h = 0.5 * x
    return h + h * jnp.tanh(h)


def _split3(x):
    hi = x.astype(BF16)
    r1 = x - hi.astype(F32)
    mid = r1.astype(BF16)
    lo = (r1 - mid.astype(F32)).astype(BF16)
    return hi, mid, lo


def _rmsnorm_kernel(x_ref, g_ref, o_ref):
    x = x_ref[...]
    y = x * lax.rsqrt(jnp.mean(x * x, axis=-1, keepdims=True) + RMS_EPS)
    o_ref[...] = (y * g_ref[...]).astype(o_ref.dtype)


def _rmsnorm(x2d, gain, out_dtype, rows=512):
    m, d = x2d.shape
    rows = min(rows, m)
    assert m % rows == 0, (m, rows)
    return pl.pallas_call(
        _rmsnorm_kernel,
        grid=(m // rows,),
        in_specs=[pl.BlockSpec((rows, d), lambda i: (i, 0)), pl.BlockSpec((1, d), lambda i: (0, 0))],
        out_specs=pl.BlockSpec((rows, d), lambda i: (i, 0)),
        out_shape=jax.ShapeDtypeStruct((m, d), out_dtype),
        compiler_params=_params("parallel"),
        name="rmsnorm",
    )(x2d, gain.reshape(1, d).astype(F32))


def _cast_kernel(x_ref, o_ref):
    o_ref[...] = x_ref[...].astype(o_ref.dtype)


CAST_BLOCK_BYTES = 8 * 1024 * 1024


def _cast_bf16(w):
    k, n = w.shape
    rows = 16
    while rows * 2 * n * 4 <= CAST_BLOCK_BYTES and k % (rows * 2) == 0:
        rows *= 2
    assert k % rows == 0
    return pl.pallas_call(
        _cast_kernel,
        grid=(k // rows,),
        in_specs=[pl.BlockSpec((rows, n), lambda i: (i, 0))],
        out_specs=pl.BlockSpec((rows, n), lambda i: (i, 0)),
        out_shape=jax.ShapeDtypeStruct((k, n), BF16),
        compiler_params=_params("parallel"),
        name="cast_bf16",
    )(w)


def _matmul_kernel(*refs, n_lhs, has_res):
    o_ref = refs[-1]
    acc = None
    for i in range(n_lhs):
        part = jnp.dot(refs[i][...], refs[n_lhs + i][...], preferred_element_type=F32)
        acc = part if acc is None else acc + part
    if has_res:
        acc = acc + refs[2 * n_lhs][...]
    o_ref[...] = acc.astype(o_ref.dtype)


def _matmul(lhs_list, w_list, out_dtype, *, n=None, wcol=None, res=None, tm=1024, tn=1024, name="matmul"):
    m = lhs_list[0].shape[0]
    n = w_list[0][0].shape[1] if n is None else n
    wcol = (lambda j: j) if wcol is None else wcol
    tm, tn = min(tm, m), min(tn, n)
    assert m % tm == 0 and n % tn == 0, (m, n, tm, tn)
    in_specs = [pl.BlockSpec((tm, a.shape[1]), lambda i, j: (i, 0)) for a in lhs_list]
    in_specs += [pl.BlockSpec((a.shape[1], tn), lambda i, j, kblk=kblk: (kblk, wcol(j)))
                 for a, (_, kblk) in zip(lhs_list, w_list)]
    args = list(lhs_list) + [w for w, _ in w_list]
    if res is not None:
        in_specs.append(pl.BlockSpec((tm, tn), lambda i, j: (i, j)))
        args.append(res)
    return pl.pallas_call(
        functools.partial(_matmul_kernel, n_lhs=len(lhs_list), has_res=res is not None),
        grid=(m // tm, n // tn),
        in_specs=in_specs,
        out_specs=pl.BlockSpec((tm, tn), lambda i, j: (i, j)),
        out_shape=jax.ShapeDtypeStruct((m, n), out_dtype),
        compiler_params=_params("parallel", "arbitrary"),
        name=name,
    )(*args)


def _band_ok(i, j, var, bq, n_back):
    qc = i >> 6
    kc = j >> 6
    kpos = var * bq - n_back * CHUNK + j
    return (kc >= qc) & (kc <= qc + n_back) & (kpos >= 0)


def _bias_a_kernel(tab_ref, o_ref, *, bq, nk, n_back):
    var = pl.program_id(1)
    for c in range(nk // LANES):
        i = lax.broadcasted_iota(jnp.int32, (bq, LANES), 0)
        j = lax.broadcasted_iota(jnp.int32, (bq, LANES), 1) + c * LANES
        idx = jnp.clip(i - j + n_back * CHUNK, -REL_CLIP, REL_CLIP) + REL_CLIP
        lane = idx & (LANES - 1)
        seg = idx >> 7
        val = jnp.zeros((bq, LANES), F32)
        for s in range(tab_ref.shape[1]):
            tab = jnp.broadcast_to(tab_ref[0, s:s + 1, :], (bq, LANES))
            val = jnp.where(seg == s, jnp.take_along_axis(tab, lane, axis=1), val)
        o_ref[0, 0, :, c * LANES:(c + 1) * LANES] = jnp.where(_band_ok(i, j, var, bq, n_back), val * LOG2E, NEG_INF)


def _bias_b_kernel(slope_ref, o_ref, *, bq, nk, n_back):
    h = pl.program_id(0)
    var = pl.program_id(1)
    i = lax.broadcasted_iota(jnp.int32, (bq, nk), 0)
    j = lax.broadcasted_iota(jnp.int32, (bq, nk), 1)
    dist = jnp.abs(i - j + n_back * CHUNK).astype(F32)
    val = (-slope_ref[h]) * dist
    o_ref[0, 0] = jnp.where(_band_ok(i, j, var, bq, n_back), val * LOG2E, NEG_INF)


def _tile_geometry(n_back, bq):
    back_rows = n_back * CHUNK
    return back_rows + bq, -(-back_rows // bq) + 1


def _window_blocks(n_back, rows):
    back_rows = n_back * CHUNK
    kb = min(rows, back_rows)
    assert rows % kb == 0 and back_rows % kb == 0
    return kb, tuple(range(-(back_rows // kb), rows // kb))


def _bias_tiles_a(rel_table, bq):
    heads, width = rel_table.shape
    nseg = -(-width // LANES)
    tab = jnp.pad(rel_table.astype(F32), ((0, 0), (0, nseg * LANES - width))).reshape(heads, nseg, LANES)
    nk, nvar = _tile_geometry(A_BACK_CHUNKS, bq)
    return pl.pallas_call(
        functools.partial(_bias_a_kernel, bq=bq, nk=nk, n_back=A_BACK_CHUNKS),
        grid=(heads, nvar),
        in_specs=[pl.BlockSpec((1, nseg, LANES), lambda h, v: (h, 0, 0))],
        out_specs=pl.BlockSpec((1, 1, bq, nk), lambda h, v: (h, v, 0, 0)),
        out_shape=jax.ShapeDtypeStruct((heads, nvar, bq, nk), F32),
        compiler_params=_params("parallel", "parallel"),
        name="bias_a",
    )(tab)


def _bias_tiles_b(heads, bq):
    slopes = 2.0 ** (-8.0 * jnp.arange(1, heads + 1, dtype=F32) / heads)
    nk, nvar = _tile_geometry(B_BACK_CHUNKS, bq)
    return pl.pallas_call(
        functools.partial(_bias_b_kernel, bq=bq, nk=nk, n_back=B_BACK_CHUNKS),
        grid=(heads, nvar),
        in_specs=[pl.BlockSpec(memory_space=pltpu.SMEM)],
        out_specs=pl.BlockSpec((1, 1, bq, nk), lambda h, v: (h, v, 0, 0)),
        out_shape=jax.ShapeDtypeStruct((heads, nvar, bq, nk), F32),
        compiler_params=_params("parallel", "parallel"),
        name="bias_b",
    )(slopes)


def _softmax_pv(q, keys, vals, biases, sink, gate):
    logits = [jnp.einsum("nqd,nkd->nqk", q, k, preferred_element_type=F32) * (ATTN_SCALE * LOG2E) + b
              for k, b in zip(keys, biases)]
    m = functools.reduce(jnp.maximum, [jnp.max(s, axis=-1, keepdims=True) for s in logits])
    if sink is not None:
        m = jnp.maximum(m, sink)
    probs = [jnp.exp2(s - m) for s in logits]
    denom = functools.reduce(jnp.add, [jnp.sum(p, axis=-1, keepdims=True) for p in probs])
    if sink is not None:
        denom = denom + jnp.exp2(sink - m)
    o = functools.reduce(jnp.add, [jnp.einsum("nqk,nkd->nqd", p.astype(BF16), v, preferred_element_type=F32)
                                   for p, v in zip(probs, vals)])
    return (o / denom) * _silu(gate)


def _head_cols(i):
    return slice(i * HEAD_DIM, (i + 1) * HEAD_DIM)


def _attn_prompt_kernel(*refs, nblk, bq, nk, ntile, hps, kvps, hpu, has_sink):
    sink_ref, q_ref = refs[0], refs[1]
    k_refs = refs[2:2 + nblk]
    v_refs = refs[2 + nblk:2 + 2 * nblk]
    g_ref = refs[2 + 2 * nblk]
    b_refs = refs[3 + 2 * nblk:3 + 2 * nblk + ntile]
    o_ref = refs[-1]
    group = hps // kvps
    head0 = pl.program_id(0) * hps
    units = [(kv, r, kv * group + c, hpu) for kv in range(kvps) for r in range(ntile) for c in range(0, group, hpu)]

    def window(blk_refs, kv):
        return jnp.concatenate([ref[:, _head_cols(kv)] for ref in blk_refs], axis=0).astype(BF16)

    def stacked(ref, r, h0, nh):
        return jnp.concatenate([ref[r * bq:(r + 1) * bq, _head_cols(h0 + g)] for g in range(nh)], axis=0)

    kwin = [window(k_refs, kv) for kv in range(kvps)]
    vwin = [window(v_refs, kv) for kv in range(kvps)]
    q = jnp.stack([stacked(q_ref, r, h0, nh) for _, r, h0, nh in units]).astype(BF16)
    k = jnp.stack([kwin[kv][r * bq:r * bq + nk] for kv, r, _, _ in units])
    v = jnp.stack([vwin[kv][r * bq:r * bq + nk] for kv, r, _, _ in units])
    bias = jnp.stack([b_refs[r][h0:h0 + nh, 0].reshape(nh * bq, nk) for _, r, h0, nh in units])
    gate = jnp.stack([stacked(g_ref, r, h0, nh) for _, r, h0, nh in units])
    sink = None
    if has_sink:
        sink = jnp.stack([jnp.concatenate([jnp.full((bq, 1), sink_ref[head0 + h0 + g], F32)
                                           for g in range(nh)], axis=0) for _, _, h0, nh in units])
    out = _softmax_pv(q, [k], [v], [bias], sink, gate).astype(o_ref.dtype)
    for n, (_, r, h0, nh) in enumerate(units):
        for g in range(nh):
            o_ref[r * bq:(r + 1) * bq, _head_cols(h0 + g)] = out[n, g * bq:(g + 1) * bq, :]


def _attn_prompt(qkv, gates, bias, sinks, *, heads, group, n_back, q_col, k_col, v_col, g_col):
    t = qkv.shape[0]
    rows, bq = ATTN_ROWS_PER_STEP, ATTN_TILE_ROWS
    ntile = rows // bq
    hps = ATTN_HEADS_PER_STEP
    assert hps % group == 0 and heads % hps == 0
    kvps = hps // group
    nk, nvar = _tile_geometry(n_back, bq)
    kb, offs = _window_blocks(n_back, rows)
    assert t % rows == 0 and bias.shape == (heads, nvar, bq, nk)
    assert q_col % hps == 0 and g_col % hps == 0 and k_col % kvps == 0 and v_col % kvps == 0
    has_sink = sinks is not None
    sink_arr = sinks.astype(F32) * LOG2E if has_sink else jnp.zeros((heads,), F32)

    def kv_spec(col, off):
        return pl.BlockSpec((kb, kvps * HEAD_DIM),
                            lambda hb, qi: (jnp.maximum(qi * (rows // kb) + off, 0), col // kvps + hb))

    def bias_spec(r):
        return pl.BlockSpec((hps, 1, bq, nk), lambda hb, qi: (hb, jnp.minimum(qi * ntile + r, nvar - 1), 0, 0))

    in_specs = [pl.BlockSpec(memory_space=pltpu.SMEM),
                pl.BlockSpec((rows, hps * HEAD_DIM), lambda hb, qi: (qi, q_col // hps + hb))]
    in_specs += [kv_spec(k_col, off) for off in offs]
    in_specs += [kv_spec(v_col, off) for off in offs]
    in_specs += [pl.BlockSpec((rows, hps * HEAD_DIM), lambda hb, qi: (qi, g_col // hps + hb))]
    in_specs += [bias_spec(r) for r in range(ntile)]
    return pl.pallas_call(
        functools.partial(_attn_prompt_kernel, nblk=len(offs), bq=bq, nk=nk, ntile=ntile, hps=hps, kvps=kvps,
                          hpu=min(group, ATTN_HEADS_PER_UNIT), has_sink=has_sink),
        grid=(heads // hps, t // rows),
        in_specs=in_specs,
        out_specs=pl.BlockSpec((rows, hps * HEAD_DIM), lambda hb, qi: (qi, hb)),
        out_shape=jax.ShapeDtypeStruct((t, heads * HEAD_DIM), BF16),
        compiler_params=_params("parallel", "arbitrary"),
        name="attn_prompt",
    )(sink_arr, qkv, *([qkv] * (2 * len(offs))), gates, *([bias] * ntile))


def _attn_step_kernel(sink_ref, qkv_ref, ck_ref, cv_ref, g_ref, b_ref, o_ref, *,
                      heads, group, cache_len, q_col, k_col, v_col, has_sink):
    t = qkv_ref.shape[0]
    kv_heads = heads // group

    def head_block(ref2d, col):
        return ref2d[:, _head_cols(col)]

    def per_kv(fn):
        return jnp.stack([fn(kv) for kv in range(kv_heads)])

    def grouped(ref2d, col0):
        return per_kv(lambda kv: jnp.concatenate(
            [head_block(ref2d, col0 + kv * group + g) for g in range(group)], axis=0))

    q = grouped(qkv_ref, q_col).astype(BF16)
    keys = (per_kv(lambda kv: ck_ref[0, :, _head_cols(kv)].astype(BF16)),
            per_kv(lambda kv: head_block(qkv_ref, k_col + kv).astype(BF16)))
    vals = (per_kv(lambda kv: cv_ref[0, :, _head_cols(kv)].astype(BF16)),
            per_kv(lambda kv: head_block(qkv_ref, v_col + kv).astype(BF16)))
    bias = b_ref[:, 0].reshape(kv_heads, group * t, b_ref.shape[3])
    biases = (bias[:, :, 0:cache_len], bias[:, :, cache_len:cache_len + t])
    sink = None
    if has_sink:
        sink = per_kv(lambda kv: jnp.concatenate(
            [jnp.full((t, 1), sink_ref[kv * group + g], F32) for g in range(group)], axis=0))
    out = _softmax_pv(q, keys, vals, biases, sink, grouped(g_ref, 0)).astype(o_ref.dtype)
    for kv in range(kv_heads):
        for g in range(group):
            h = kv * group + g
            o_ref[:, h * HEAD_DIM:(h + 1) * HEAD_DIM] = out[kv, g * t:(g + 1) * t, :]


def _attn_step(qkv, cache_k, cache_v, gates, bias, sinks, *, batch, heads, group, q_col, k_col, v_col, g_blk):
    rows = qkv.shape[0]
    t = rows // batch
    cache_len, kvw = cache_k.shape[1], cache_k.shape[2]
    nvar, bq, nk = bias.shape[1:]
    assert t <= CHUNK and cache_len + t <= nk and t <= bq
    has_sink = sinks is not None
    sink_arr = sinks.astype(F32) * LOG2E if has_sink else jnp.zeros((heads,), F32)
    width = heads * HEAD_DIM
    return pl.pallas_call(
        functools.partial(_attn_step_kernel, heads=heads, group=group, cache_len=cache_len,
                          q_col=q_col, k_col=k_col, v_col=v_col, has_sink=has_sink),
        grid=(batch,),
        in_specs=[pl.BlockSpec(memory_space=pltpu.SMEM),
                  pl.BlockSpec((t, qkv.shape[1]), lambda b: (b, 0)),
                  pl.BlockSpec((1, cache_len, kvw), lambda b: (b, 0, 0)),
                  pl.BlockSpec((1, cache_len, kvw), lambda b: (b, 0, 0)),
                  pl.BlockSpec((t, width), lambda b: (b, g_blk)),
                  pl.BlockSpec((heads, 1, t, nk), lambda b: (0, nvar - 1, 0, 0))],
        out_specs=pl.BlockSpec((t, width), lambda b: (b, 0)),
        out_shape=jax.ShapeDtypeStruct((rows, width), BF16),
        compiler_params=_params("parallel"),
        name="attn_step",
    )(sink_arr, qkv, cache_k, cache_v, gates, bias)


CONV_PAD = SUBLANES
CONV_STRIP = 4 * LANES


def _causal_dwconv_strip(pad_s, w_ref, s0, rows):
    cols = slice(s0, s0 + CONV_STRIP)
    xs = pad_s[:, cols]
    acc = w_ref[0:1, cols] * xs
    for k in range(1, w_ref.shape[0]):
        acc = w_ref[k:k + 1, cols] * xs + pltpu.roll(acc, 1, 0)
    return acc[CONV_PAD:CONV_PAD + rows]


def _ssd_kernel(xbc_ref, z_ref, dtr_ref, g_ref, b_ref, c2_ref, h2_ref, conv0_ref, dconv0_ref, h0_ref, cw_ref, cb_ref,
                dw_ref, dtb_ref, alog_ref, dsk_ref, cng_ref, exp_ref,
                y_ref, yd_ref, hout_ref, convout_ref, dconvout_ref, h_s, xpad_s, upad_s, act_s, gated_s, *, rows):
    c = pl.program_id(1)
    q = SSD_Q
    width = y_ref.shape[2]
    gw = width // C_GROUPS
    pad0 = CONV_PAD

    @pl.when(c == 0)
    def _():
        h_s[...] = h0_ref[0]
        xpad_s[0:pad0, :] = conv0_ref[0]
        upad_s[0:pad0, :] = dconv0_ref[0]

    upad_s[pad0:pad0 + rows, :] = c2_ref[0] * h2_ref[0]
    for s0 in range(0, upad_s.shape[1], CONV_STRIP):
        cols = slice(s0, s0 + CONV_STRIP)
        v = _causal_dwconv_strip(upad_s, dw_ref, s0, rows)
        yd_ref[0, :, cols] = (b_ref[0, :, cols] * v * _silu(g_ref[0, :, cols])).astype(yd_ref.dtype)
    dconvout_ref[0] = upad_s[rows:rows + pad0, :]
    upad_s[0:pad0, :] = upad_s[rows:rows + pad0, :]

    xpad_s[pad0:pad0 + rows, :] = xbc_ref[0]
    for s0 in range(0, xpad_s.shape[1], CONV_STRIP):
        cols = slice(s0, s0 + CONV_STRIP)
        act_s[0:rows, cols] = _silu(_causal_dwconv_strip(xpad_s, cw_ref, s0, rows) + cb_ref[:, cols])
    if rows < q:
        act_s[rows:q, :] = jnp.zeros((q - rows, act_s.shape[1]), F32)
    convout_ref[0] = xpad_s[rows:rows + pad0, :]
    xpad_s[0:pad0, :] = xpad_s[rows:rows + pad0, :]

    row = lax.broadcasted_iota(jnp.int32, (q, LANES), 0)
    lane = lax.broadcasted_iota(jnp.int32, (q, LANES), 1)
    causal = row >= (lane & (C_HEAD_DIM - 1))
    low_half = lane < C_HEAD_DIM
    diag_g = (lax.broadcasted_iota(jnp.int32, (q, gw), 0)
              == (lax.broadcasted_iota(jnp.int32, (q, gw), 1) & (C_HEAD_DIM - 1)))
    tril = (lax.broadcasted_iota(jnp.int32, (q, q), 0) >= lax.broadcasted_iota(jnp.int32, (q, q), 1)).astype(BF16)
    tril3 = jnp.concatenate([tril] * 3, axis=1)
    neg_a = -jnp.exp(alog_ref[...])
    valid = min(rows, q)

    def pad_rows(v):
        return v if valid == q else jnp.concatenate([v, jnp.zeros((q - valid, v.shape[1]), F32)], axis=0)

    for r0 in range(0, rows, q):
        dt = jnp.where(row < valid, jax.nn.softplus(pad_rows(dtr_ref[0, r0:r0 + valid, :]) + dtb_ref[...]), 0.0)
        cum = jnp.dot(tril3, jnp.concatenate(_split3(dt * neg_a), axis=0), preferred_element_type=F32)
        spread = jnp.concatenate(_split3(jnp.concatenate([cum, dt], axis=0)), axis=1)
        ssq = jnp.zeros((q, 1), F32)
        for g in range(C_GROUPS):
            cols = slice(g * gw, (g + 1) * gw)
            ex = jnp.dot(spread, exp_ref[:, cols], preferred_element_type=F32)
            cexp, dtx = ex[0:q], ex[q:2 * q]
            x_g = act_s[r0:r0 + q, cols]
            b_g = act_s[r0:r0 + q, width + g * C_STATE:width + (g + 1) * C_STATE].astype(BF16)
            c_g = act_s[r0:r0 + q, width + (C_GROUPS + g) * C_STATE:width + (C_GROUPS + g + 1) * C_STATE].astype(BF16)
            xdt = x_g * dtx
            cum_row = jnp.sum(jnp.where(diag_g, cexp, 0.0), axis=0, keepdims=True)
            cum_last = cexp[q - 1:q, :]
            scores = lax.dot_general(c_g, jnp.concatenate([b_g, b_g], axis=0), (((1,), (1,)), ((), ())),
                                     preferred_element_type=F32)
            h_g = h_s[:, cols]
            y_off = jnp.dot(c_g, h_g.astype(BF16), preferred_element_type=F32) * jnp.exp(cexp)
            y_diag = []
            for j in range(gw // LANES):
                pc = slice(j * LANES, (j + 1) * LANES)
                seg = cexp[:, pc] - cum_row[:, pc]
                lmat = jnp.where(causal, jnp.exp(jnp.where(causal, seg, 0.0)), 0.0)
                xp = xdt[:, pc]
                blockdiag = jnp.concatenate([jnp.where(low_half, xp, 0.0), jnp.where(low_half, 0.0, xp)], axis=0)
                y_diag.append(jnp.dot((scores * lmat).astype(BF16), blockdiag.astype(BF16),
                                      preferred_element_type=F32))
            y = jnp.concatenate(y_diag, axis=1) + y_off + dsk_ref[:, cols] * x_g
            xw = (xdt * jnp.exp(cum_last - cexp)).astype(BF16)
            h_s[:, cols] = h_g * jnp.exp(cum_last) + lax.dot_general(
                b_g, xw, (((0,), (0,)), ((), ())), preferred_element_type=F32)
            gated = y * _silu(pad_rows(z_ref[0, r0:r0 + valid, cols]))
            gated_s[:, cols] = gated
            ssq = ssq + jnp.sum(gated * gated, axis=-1, keepdims=True)
        inv = lax.rsqrt(ssq * (1.0 / width) + RMS_EPS)
        y_ref[0, r0:r0 + valid, :] = ((gated_s[...] * inv)[0:valid] * cng_ref[...]).astype(y_ref.dtype)

    @pl.when(c == pl.num_programs(1) - 1)
    def _():
        hout_ref[0] = h_s[...]


def _ssd(xbc, z, dtr, gbch, states, p, *, rows):
    b, t, conv_dim = xbc.shape
    width = z.shape[2]
    d_width = gbch.shape[2] // 4
    n_heads = width // C_HEAD_DIM
    ssm_state, conv_c_state, conv_d_state = states
    assert t % rows == 0 and rows % 16 == 0 and (rows <= SSD_Q or rows % SSD_Q == 0) and n_heads <= LANES
    assert conv_dim % CONV_STRIP == 0 and d_width % CONV_STRIP == 0
    pad_lanes = lambda v: jnp.pad(v.astype(F32), (0, LANES - v.shape[0])).reshape(1, LANES)
    pad_hist = lambda s: jnp.pad(s.astype(F32), ((0, 0), (CONV_PAD - s.shape[1], 0), (0, 0)))
    h0t = jnp.swapaxes(ssm_state.astype(F32).reshape(b, width, C_STATE), 1, 2)
    expand = (jnp.arange(LANES)[:, None] == (jnp.arange(width)[None, :] // C_HEAD_DIM)).astype(BF16)
    expand3 = jnp.concatenate([expand] * 3, axis=0)
    full = lambda shape: pl.BlockSpec(shape, lambda bi, ci: (0,) * len(shape))
    per_b = lambda shape: pl.BlockSpec(shape, lambda bi, ci: (bi,) + (0,) * (len(shape) - 1))
    blk = lambda w, k=0: pl.BlockSpec((1, rows, w), lambda bi, ci: (bi, ci, k))
    y, yd, h_out, conv_out, dconv_out = pl.pallas_call(
        functools.partial(_ssd_kernel, rows=rows),
        grid=(b, t // rows),
        in_specs=[blk(conv_dim), blk(width), blk(LANES),
                  blk(d_width, 0), blk(d_width, 1), blk(d_width, 2), blk(d_width, 3),
                  per_b((1, CONV_PAD, conv_dim)), per_b((1, CONV_PAD, d_width)), per_b((1, C_STATE, width)),
                  full((C_CONV_W, conv_dim)), full((1, conv_dim)), full((D_CONV_W, d_width)),
                  full((1, LANES)), full((1, LANES)), full((1, width)), full((1, width)),
                  full((3 * LANES, width))],
        out_specs=[blk(width), blk(d_width), per_b((1, C_STATE, width)),
                   per_b((1, CONV_PAD, conv_dim)), per_b((1, CONV_PAD, d_width))],
        out_shape=[jax.ShapeDtypeStruct((b, t, width), BF16),
                   jax.ShapeDtypeStruct((b, t, d_width), BF16),
                   jax.ShapeDtypeStruct((b, C_STATE, width), F32),
                   jax.ShapeDtypeStruct((b, CONV_PAD, conv_dim), F32),
                   jax.ShapeDtypeStruct((b, CONV_PAD, d_width), F32)],
        scratch_shapes=[pltpu.VMEM((C_STATE, width), F32),
                        pltpu.VMEM((CONV_PAD + rows, conv_dim), F32),
                        pltpu.VMEM((CONV_PAD + rows, d_width), F32),
                        pltpu.VMEM((max(rows, SSD_Q), conv_dim), F32),
                        pltpu.VMEM((SSD_Q, width), F32)],
        compiler_params=_params("parallel", "arbitrary"),
        name="ssd",
    )(xbc, z, dtr, gbch, gbch, gbch, gbch, pad_hist(conv_c_state), pad_hist(conv_d_state), h0t,
      p["conv_c_w"].astype(F32), p["conv_c_b"].astype(F32).reshape(1, conv_dim), p["conv_d_w"].astype(F32),
      pad_lanes(p["dt_bias"]), pad_lanes(p["a_log"]),
      jnp.repeat(p["d_skip"].astype(F32), C_HEAD_DIM).reshape(1, width),
      p["c_norm"].astype(F32).reshape(1, width), expand3)
    new_state = jnp.swapaxes(h_out, 1, 2).reshape(b, n_heads, C_HEAD_DIM, C_STATE)
    return (y, yd, new_state, conv_out[:, CONV_PAD - (C_CONV_W - 1):, :],
            dconv_out[:, CONV_PAD - (D_CONV_W - 1):, :])


TN_WIDE = 1024
TN_NARROW = 512


def _even_weights(w_in, w_out):
    aw = w_out.shape[0] // 2
    bkv = (w_in.shape[1] - 6 * aw) // 2
    assert aw % TN_WIDE == 0 and (2 * bkv) % TN_NARROW == 0 and aw % TN_NARROW == 0
    return dict(w_in=_cast_bf16(w_in), w_out=_cast_bf16(w_out), aw=aw, bkv=bkv)


def _in_proj_even(hn, w, kv_dtype):
    aw, wb = w["aw"], w["w_in"]
    qkv_a = _matmul([hn], [(wb, 0)], kv_dtype, n=3 * aw, tn=TN_WIDE, name="in_proj_qkv_a")
    nq = aw // TN_NARROW
    qkv_b = _matmul([hn], [(wb, 0)], kv_dtype, n=aw + 2 * w["bkv"], tn=TN_NARROW, name="in_proj_qkv_b",
                    wcol=lambda j: jnp.where(j < nq, 4 * nq + j, 6 * nq + j - nq))
    ng = aw // TN_WIDE
    gates = _matmul([hn], [(wb, 0)], F32, n=2 * aw, tn=TN_WIDE, name="in_proj_gates",
                    wcol=lambda j: jnp.where(j < ng, 3 * ng + j, 5 * ng + j - ng))
    return qkv_a, qkv_b, gates


def _attn_layer(x2d, hn, w, bias_a, bias_b, sinks, caches, batch):
    aw, bkv = w["aw"], w["bkv"]
    heads = aw // HEAD_DIM
    qkv_a, qkv_b, gates = _in_proj_even(hn, w, BF16 if caches is None else F32)
    kvh_b = bkv // HEAD_DIM
    if caches is None:
        oa = _attn_prompt(qkv_a, gates, bias_a, None, heads=heads, group=1, n_back=A_BACK_CHUNKS,
                          q_col=0, k_col=heads, v_col=2 * heads, g_col=0)
        ob = _attn_prompt(qkv_b, gates, bias_b, sinks, heads=heads, group=heads // kvh_b, n_back=B_BACK_CHUNKS,
                          q_col=0, k_col=heads, v_col=heads + kvh_b, g_col=heads)
    else:
        ck_a, cv_a, ck_b, cv_b = caches
        flat = lambda c: c.astype(BF16).reshape(c.shape[0], c.shape[1], -1)
        oa = _attn_step(qkv_a, flat(ck_a), flat(cv_a), gates, bias_a, None, batch=batch, heads=heads, group=1,
                        q_col=0, k_col=heads, v_col=2 * heads, g_blk=0)
        ob = _attn_step(qkv_b, flat(ck_b), flat(cv_b), gates, bias_b, sinks, batch=batch, heads=heads,
                        group=heads // kvh_b, q_col=0, k_col=heads, v_col=heads + kvh_b, g_blk=1)
    x_new = _matmul([oa, ob], [(w["w_out"], 0), (w["w_out"], 1)], F32, res=x2d, tn=TN_WIDE, name="out_proj_even")
    return x_new, qkv_a, qkv_b


def _odd_weights(w_in, w_out, c_width, conv_dim, c_heads):
    o1 = c_width + conv_dim
    assert c_width % TN_WIDE == 0 and conv_dim % TN_WIDE == 0
    wb = _cast_bf16(w_in)
    return dict(
        w_in=wb,
        dt=jnp.pad(wb[:, o1:o1 + c_heads], ((0, 0), (0, LANES - c_heads))),
        gbch=wb[:, o1 + c_heads:],
        w_out=_cast_bf16(w_out))


def _ssm_conv_layer(x2d, hn, w, p, states, batch, rows):
    c_width = p["c_norm"].shape[0]
    conv_dim = p["conv_c_w"].shape[1]
    d_width = p["conv_d_w"].shape[1]
    t = x2d.shape[0] // batch
    nz = c_width // TN_WIDE
    z = _matmul([hn], [(w["w_in"], 0)], F32, n=c_width, tn=TN_WIDE, name="in_proj_z").reshape(batch, t, c_width)
    xbc = _matmul([hn], [(w["w_in"], 0)], F32, n=conv_dim, tn=TN_WIDE, wcol=lambda j: nz + j,
                  name="in_proj_xbc").reshape(batch, t, conv_dim)
    dtr = _matmul([hn], [(w["dt"], 0)], F32, name="in_proj_dt").reshape(batch, t, LANES)
    gbch = _matmul([hn], [(w["gbch"], 0)], F32, tn=TN_WIDE, name="in_proj_gbch").reshape(batch, t, -1)
    yc, yd, new_ssm, new_cc, new_dc = _ssd(xbc, z, dtr, gbch, states, p, rows=rows)
    assert c_width % d_width == 0
    x_new = _matmul([yc.reshape(batch * t, c_width), yd.reshape(batch * t, d_width)],
                    [(w["w_out"], 0), (w["w_out"], c_width // d_width)], F32,
                    res=x2d, tn=TN_NARROW, name="out_proj_odd")
    return x_new, new_ssm, new_cc, new_dc


def kernel(x_prompt, x_sample, cache_a_k, cache_a_v, cache_b_k, cache_b_v, state_c_ssm, state_c_conv, state_d_conv, norm_even, w_in_even, w_out_even, rel_bias_a, sinks_b, norm_odd, w_in_odd, w_out_odd, conv_c_w, conv_c_b, dt_bias, a_log, d_skip, c_norm, conv_d_w, final_norm):
    bp, tp, d = x_prompt.shape
    bs, ts, _ = x_sample.shape
    assert bp == 1, "the prompt attention kernel handles one prompt sequence"
    dt_out = x_prompt.dtype
    depth = norm_even.shape[0] + norm_odd.shape[0]
    xp = x_prompt.astype(F32).reshape(bp * tp, d)
    xs = x_sample.astype(F32).reshape(bs * ts, d)
    keep_a = min(A_BACK_CHUNKS * CHUNK, tp)
    keep_b = min(B_BACK_CHUNKS * CHUNK, tp)
    outs = {k: [] for k in ("p_ak", "p_av", "p_bk", "p_bv", "p_ssm", "p_cc", "p_dc",
                            "s_ak", "s_av", "s_bk", "s_bv", "s_ssm", "s_cc", "s_dc")}
    for layer in range(depth):
        i = layer // 2
        if layer % 2 == 0:
            w = _even_weights(w_in_even[i], w_out_even[i])
            aw, bkv = w["aw"], w["bkv"]
            heads = aw // HEAD_DIM
            bias_a = _bias_tiles_a(rel_bias_a[i], ATTN_TILE_ROWS)
            bias_b = _bias_tiles_b(heads, ATTN_TILE_ROWS)
            hp = _rmsnorm(xp, norm_even[i], BF16)
            hs = _rmsnorm(xs, norm_even[i], BF16)
            xp_new, _, _ = _attn_layer(xp, hp, w, bias_a, bias_b, sinks_b[i], None, bp)
            kv_a = _matmul([hp[tp - keep_a:]], [(w["w_in"], 0)], F32, n=2 * aw, tn=TN_WIDE,
                           wcol=lambda j: aw // TN_WIDE + j, name="cache_kv_a")
            kv_b = _matmul([hp[tp - keep_b:]], [(w["w_in"], 0)], F32, n=2 * bkv, tn=TN_NARROW,
                           wcol=lambda j: 6 * aw // TN_NARROW + j, name="cache_kv_b")
            outs["p_ak"].append(kv_a[:, :aw].reshape(bp, keep_a, heads, HEAD_DIM))
            outs["p_av"].append(kv_a[:, aw:].reshape(bp, keep_a, heads, HEAD_DIM))
            outs["p_bk"].append(kv_b[:, :bkv].reshape(bp, keep_b, bkv // HEAD_DIM, HEAD_DIM))
            outs["p_bv"].append(kv_b[:, bkv:].reshape(bp, keep_b, bkv // HEAD_DIM, HEAD_DIM))
            xs_new, qkv_a, qkv_b = _attn_layer(xs, hs, w, bias_a, bias_b, sinks_b[i],
                                               (cache_a_k[i], cache_a_v[i], cache_b_k[i], cache_b_v[i]), bs)
            outs["s_ak"].append(qkv_a[:, aw:2 * aw].reshape(bs, ts, heads, HEAD_DIM))
            outs["s_av"].append(qkv_a[:, 2 * aw:].reshape(bs, ts, heads, HEAD_DIM))
            outs["s_bk"].append(qkv_b[:, aw:aw + bkv].reshape(bs, ts, bkv // HEAD_DIM, HEAD_DIM))
            outs["s_bv"].append(qkv_b[:, aw + bkv:].reshape(bs, ts, bkv // HEAD_DIM, HEAD_DIM))
            xp, xs = xp_new, xs_new
        else:
            c_width = c_norm.shape[1]
            conv_dim = conv_c_w.shape[2]
            c_heads = dt_bias.shape[1]
            d_width = conv_d_w.shape[2]
            w = _odd_weights(w_in_odd[i], w_out_odd[i], c_width, conv_dim, c_heads)
            p = dict(conv_c_w=conv_c_w[i], conv_c_b=conv_c_b[i], dt_bias=dt_bias[i], a_log=a_log[i],
                     d_skip=d_skip[i], c_norm=c_norm[i], conv_d_w=conv_d_w[i])
            zero = (jnp.zeros((bp, c_heads, C_HEAD_DIM, C_STATE), F32),
                    jnp.zeros((bp, C_CONV_W - 1, conv_dim), F32),
                    jnp.zeros((bp, D_CONV_W - 1, d_width), F32))
            hp = _rmsnorm(xp, norm_odd[i], BF16)
            hs = _rmsnorm(xs, norm_odd[i], BF16)
            xp, ssm, cc, dc = _ssm_conv_layer(xp, hp, w, p, zero, bp, min(SSD_ROWS_PER_STEP, tp))
            xs, sssm, scc, sdc = _ssm_conv_layer(xs, hs, w, p, (state_c_ssm[i], state_c_conv[i], state_d_conv[i]),
                                                 bs, ts)
            outs["p_ssm"].append(ssm.astype(dt_out)); outs["p_cc"].append(cc.astype(dt_out))
            outs["p_dc"].append(dc.astype(dt_out))
            outs["s_ssm"].append(sssm.astype(dt_out)); outs["s_cc"].append(scc.astype(dt_out))
            outs["s_dc"].append(sdc.astype(dt_out))
    y_prompt = _rmsnorm(xp, final_norm, dt_out).reshape(bp, tp, d)
    y_sample = _rmsnorm(xs, final_norm, dt_out).reshape(bs, ts, d)
    st = lambda k: jnp.stack(outs[k])
    return (y_prompt, y_sample, st("p_ak"), st("p_av"), st("p_bk"), st("p_bv"), st("p_ssm"), st("p_cc"), st("p_dc"),
            st("s_ak"), st("s_av"), st("s_bk"), st("s_bv"), st("s_ssm"), st("s_cc"), st("s_dc"))
```

```python
import functools

import jax
import jax.numpy as jnp
from jax import lax
from jax.experimental import pallas as pl
from jax.experimental.pallas import tpu as pltpu

F32 = jnp.float32
BF16 = jnp.bfloat16

CHUNK = 64
HEAD_DIM = 128
A_BACK_CHUNKS = 8
B_BACK_CHUNKS = 2
B_GROUP = 8
REL_CLIP = 128
C_HEAD_DIM = 64
C_GROUPS = 8
C_STATE = 128
C_CONV_W = 4
D_CONV_W = 3
RMS_EPS = 1e-6
NEG_INF = -1e30
ATTN_SCALE = HEAD_DIM ** -0.5
LOG2E = 1.4426950408889634

V7X_VMEM_BYTES = 64 * 1024 * 1024
VMEM_LIMIT_BYTES = V7X_VMEM_BYTES - 8 * 1024 * 1024
LANES = 128
SUBLANES = 8

ATTN_TILE_ROWS = 128
ATTN_ROWS_PER_STEP = 512
ATTN_HEADS_PER_STEP = 8
ATTN_HEADS_PER_UNIT = 1
SSD_Q = CHUNK
SSD_ROWS_PER_STEP = 2 * SSD_Q


def _params(*sem):
    return pltpu.CompilerParams(dimension_semantics=sem, vmem_limit_bytes=VMEM_LIMIT_BYTES)


def _silu(x):
    h = 0.5 * x
    return h + h * jnp.tanh(h)


def _split3(x):
    hi = x.astype(BF16)
    r1 = x - hi.astype(F32)
    mid = r1.astype(BF16)
    lo = (r1 - mid.astype(F32)).astype(BF16)
    return hi, mid, lo


def _rmsnorm_kernel(x_ref, g_ref, o_ref):
    x = x_ref[...]
    y = x * lax.rsqrt(jnp.mean(x * x, axis=-1, keepdims=True) + RMS_EPS)
    o_ref[...] = (y * g_ref[...]).astype(o_ref.dtype)


def _rmsnorm(x2d, gain, out_dtype, rows=512):
    m, d = x2d.shape
    rows = min(rows, m)
    assert m % rows == 0, (m, rows)
    return pl.pallas_call(
        _rmsnorm_kernel,
        grid=(m // rows,),
        in_specs=[pl.BlockSpec((rows, d), lambda i: (i, 0)), pl.BlockSpec((1, d), lambda i: (0, 0))],
        out_specs=pl.BlockSpec((rows, d), lambda i: (i, 0)),
        out_shape=jax.ShapeDtypeStruct((m, d), out_dtype),
        compiler_params=_params("parallel"),
        name="rmsnorm",
    )(x2d, gain.reshape(1, d).astype(F32))


def _matmul_kernel(*refs, n_lhs, has_res):
    o_ref = refs[-1]
    acc = None
    for i in range(n_lhs):
        part = jnp.dot(refs[i][...], refs[n_lhs + i][...], preferred_element_type=F32)
        acc = part if acc is None else acc + part
    if has_res:
        acc = acc + refs[2 * n_lhs][...]
    o_ref[...] = acc.astype(o_ref.dtype)


def _matmul(lhs_list, w_list, out_dtype, *, n=None, wcol=None, res=None, tm=1024, tn=1024, name="matmul"):
    m = lhs_list[0].shape[0]
    n = w_list[0][0].shape[1] if n is None else n
    wcol = (lambda j: j) if wcol is None else wcol
    tm, tn = min(tm, m), min(tn, n)
    assert m % tm == 0 and n % tn == 0, (m, n, tm, tn)
    in_specs = [pl.BlockSpec((tm, a.shape[1]), lambda i, j: (i, 0)) for a in lhs_list]
    in_specs += [pl.BlockSpec((a.shape[1], tn), lambda i, j, kblk=kblk: (kblk, wcol(j)))
                 for a, (_, kblk) in zip(lhs_list, w_list)]
    args = list(lhs_list) + [w for w, _ in w_list]
    if res is not None:
        in_specs.append(pl.BlockSpec((tm, tn), lambda i, j: (i, j)))
        args.append(res)
    return pl.pallas_call(
        functools.partial(_matmul_kernel, n_lhs=len(lhs_list), has_res=res is not None),
        grid=(m // tm, n // tn),
        in_specs=in_specs,
        out_specs=pl.BlockSpec((tm, tn), lambda i, j: (i, j)),
        out_shape=jax.ShapeDtypeStruct((m, n), out_dtype),
        compiler_params=_params("parallel", "arbitrary"),
        name=name,
    )(*args)


def _band_ok(i, j, var, bq, n_back):
    qc = i >> 6
    kc = j >> 6
    kpos = var * bq - n_back * CHUNK + j
    return (kc >= qc) & (kc <= qc + n_back) & (kpos >= 0)


def _bias_a_kernel(tab_ref, o_ref, *, bq, nk, n_back):
    var = pl.program_id(1)
    for c in range(nk // LANES):
        i = lax.broadcasted_iota(jnp.int32, (bq, LANES), 0)
        j = lax.broadcasted_iota(jnp.int32, (bq, LANES), 1) + c * LANES
        idx = jnp.clip(i - j + n_back * CHUNK, -REL_CLIP, REL_CLIP) + REL_CLIP
        lane = idx & (LANES - 1)
        seg = idx >> 7
        val = jnp.zeros((bq, LANES), F32)
        for s in range(tab_ref.shape[1]):
            tab = jnp.broadcast_to(tab_ref[0, s:s + 1, :], (bq, LANES))
            val = jnp.where(seg == s, jnp.take_along_axis(tab, lane, axis=1), val)
        o_ref[0, 0, :, c * LANES:(c + 1) * LANES] = jnp.where(_band_ok(i, j, var, bq, n_back), val * LOG2E, NEG_INF)


def _bias_b_kernel(slope_ref, o_ref, *, bq, nk, n_back):
    h = pl.program_id(0)
    var = pl.program_id(1)
    i = lax.broadcasted_iota(jnp.int32, (bq, nk), 0)
    j = lax.broadcasted_iota(jnp.int32, (bq, nk), 1)
    dist = jnp.abs(i - j + n_back * CHUNK).astype(F32)
    val = (-slope_ref[h]) * dist
    o_ref[0, 0] = jnp.where(_band_ok(i, j, var, bq, n_back), val * LOG2E, NEG_INF)


def _tile_geometry(n_back, bq):
    back_rows = n_back * CHUNK
    return back_rows + bq, -(-back_rows // bq) + 1


def _window_blocks(n_back, rows):
    back_rows = n_back * CHUNK
    kb = min(rows, back_rows)
    assert rows % kb == 0 and back_rows % kb == 0
    return kb, tuple(range(-(back_rows // kb), rows // kb))


def _bias_tiles_a(rel_table, bq):
    heads, width = rel_table.shape
    nseg = -(-width // LANES)
    tab = jnp.pad(rel_table.astype(F32), ((0, 0), (0, nseg * LANES - width))).reshape(heads, nseg, LANES)
    nk, nvar = _tile_geometry(A_BACK_CHUNKS, bq)
    return pl.pallas_call(
        functools.partial(_bias_a_kernel, bq=bq, nk=nk, n_back=A_BACK_CHUNKS),
        grid=(heads, nvar),
        in_specs=[pl.BlockSpec((1, nseg, LANES), lambda h, v: (h, 0, 0))],
        out_specs=pl.BlockSpec((1, 1, bq, nk), lambda h, v: (h, v, 0, 0)),
        out_shape=jax.ShapeDtypeStruct((heads, nvar, bq, nk), F32),
        compiler_params=_params("parallel", "parallel"),
        name="bias_a",
    )(tab)


def _bias_tiles_b(heads, bq):
    slopes = 2.0 ** (-8.0 * jnp.arange(1, heads + 1, dtype=F32) / heads)
    nk, nvar = _tile_geometry(B_BACK_CHUNKS, bq)
    return pl.pallas_call(
        functools.partial(_bias_b_kernel, bq=bq, nk=nk, n_back=B_BACK_CHUNKS),
        grid=(heads, nvar),
        in_specs=[pl.BlockSpec(memory_space=pltpu.SMEM)],
        out_specs=pl.BlockSpec((1, 1, bq, nk), lambda h, v: (h, v, 0, 0)),
        out_shape=jax.ShapeDtypeStruct((heads, nvar, bq, nk), F32),
        compiler_params=_params("parallel", "parallel"),
        name="bias_b",
    )(slopes)


def _softmax_pv(q, keys, vals, biases, sink, gate):
    logits = [jnp.einsum("nqd,nkd->nqk", q, k, preferred_element_type=F32) * (ATTN_SCALE * LOG2E) + b
              for k, b in zip(keys, biases)]
    m = functools.reduce(jnp.maximum, [jnp.max(s, axis=-1, keepdims=True) for s in logits])
    if sink is not None:
        m = jnp.maximum(m, sink)
    probs = [jnp.exp2(s - m) for s in logits]
    denom = functools.reduce(jnp.add, [jnp.sum(p, axis=-1, keepdims=True) for p in probs])
    if sink is not None:
        denom = denom + jnp.exp2(sink - m)
    o = functools.reduce(jnp.add, [jnp.einsum("nqk,nkd->nqd", p.astype(BF16), v, preferred_element_type=F32)
                                   for p, v in zip(probs, vals)])
    return (o / denom) * _silu(gate)


def _head_cols(i):
    return slice(i * HEAD_DIM, (i + 1) * HEAD_DIM)


def _attn_prompt_kernel(*refs, nblk, bq, nk, ntile, hps, kvps, hpu, has_sink):
    sink_ref, q_ref = refs[0], refs[1]
    k_refs = refs[2:2 + nblk]
    v_refs = refs[2 + nblk:2 + 2 * nblk]
    g_ref = refs[2 + 2 * nblk]
    b_refs = refs[3 + 2 * nblk:3 + 2 * nblk + ntile]
    o_ref = refs[-1]
    group = hps // kvps
    head0 = pl.program_id(0) * hps
    units = [(kv, r, kv * group + c, hpu) for kv in range(kvps) for r in range(ntile) for c in range(0, group, hpu)]

    def window(blk_refs, kv):
        return jnp.concatenate([ref[:, _head_cols(kv)] for ref in blk_refs], axis=0).astype(BF16)

    def stacked(ref, r, h0, nh):
        return jnp.concatenate([ref[r * bq:(r + 1) * bq, _head_cols(h0 + g)] for g in range(nh)], axis=0)

    kwin = [window(k_refs, kv) for kv in range(kvps)]
    vwin = [window(v_refs, kv) for kv in range(kvps)]
    q = jnp.stack([stacked(q_ref, r, h0, nh) for _, r, h0, nh in units]).astype(BF16)
    k = jnp.stack([kwin[kv][r * bq:r * bq + nk] for kv, r, _, _ in units])
    v = jnp.stack([vwin[kv][r * bq:r * bq + nk] for kv, r, _, _ in units])
    bias = jnp.stack([b_refs[r][h0:h0 + nh, 0].reshape(nh * bq, nk) for _, r, h0, nh in units])
    gate = jnp.stack([stacked(g_ref, r, h0, nh) for _, r, h0, nh in units])
    sink = None
    if has_sink:
        sink = jnp.stack([jnp.concatenate([jnp.full((bq, 1), sink_ref[head0 + h0 + g], F32)
                                           for g in range(nh)], axis=0) for _, _, h0, nh in units])
    out = _softmax_pv(q, [k], [v], [bias], sink, gate).astype(o_ref.dtype)
    for n, (_, r, h0, nh) in enumerate(units):
        for g in range(nh):
            o_ref[r * bq:(r + 1) * bq, _head_cols(h0 + g)] = out[n, g * bq:(g + 1) * bq, :]


def _attn_prompt(qkv, gates, bias, sinks, *, heads, group, n_back, q_col, k_col, v_col, g_col):
    t = qkv.shape[0]
    rows, bq = ATTN_ROWS_PER_STEP, ATTN_TILE_ROWS
    ntile = rows // bq
    hps = ATTN_HEADS_PER_STEP
    assert hps % group == 0 and heads % hps == 0
    kvps = hps // group
    nk, nvar = _tile_geometry(n_back, bq)
    kb, offs = _window_blocks(n_back, rows)
    assert t % rows == 0 and bias.shape == (heads, nvar, bq, nk)
    assert q_col % hps == 0 and g_col % hps == 0 and k_col % kvps == 0 and v_col % kvps == 0
    has_sink = sinks is not None
    sink_arr = sinks.astype(F32) * LOG2E if has_sink else jnp.zeros((heads,), F32)

    def kv_spec(col, off):
        return pl.BlockSpec((kb, kvps * HEAD_DIM),
                            lambda hb, qi: (jnp.maximum(qi * (rows // kb) + off, 0), col // kvps + hb))

    def bias_spec(r):
        return pl.BlockSpec((hps, 1, bq, nk), lambda hb, qi: (hb, jnp.minimum(qi * ntile + r, nvar - 1), 0, 0))

    in_specs = [pl.BlockSpec(memory_space=pltpu.SMEM),
                pl.BlockSpec((rows, hps * HEAD_DIM), lambda hb, qi: (qi, q_col // hps + hb))]
    in_specs += [kv_spec(k_col, off) for off in offs]
    in_specs += [kv_spec(v_col, off) for off in offs]
    in_specs += [pl.BlockSpec((rows, hps * HEAD_DIM), lambda hb, qi: (qi, g_col // hps + hb))]
    in_specs += [bias_spec(r) for r in range(ntile)]
    return pl.pallas_call(
        functools.partial(_attn_prompt_kernel, nblk=len(offs), bq=bq, nk=nk, ntile=ntile, hps=hps, kvps=kvps,
                          hpu=min(group, ATTN_HEADS_PER_UNIT), has_sink=has_sink),
        grid=(heads // hps, t // rows),
        in_specs=in_specs,
        out_specs=pl.BlockSpec((rows, hps * HEAD_DIM), lambda hb, qi: (qi, hb)),
        out_shape=jax.ShapeDtypeStruct((t, heads * HEAD_DIM), BF16),
        compiler_params=_params("parallel", "arbitrary"),
        name="attn_prompt",
    )(sink_arr, qkv, *([qkv] * (2 * len(offs))), gates, *([bias] * ntile))


def _attn_step_kernel(sink_ref, qkv_ref, ck_ref, cv_ref, g_ref, b_ref, o_ref, *,
                      heads, group, cache_len, q_col, k_col, v_col, has_sink):
    t = qkv_ref.shape[0]
    kv_heads = heads // group

    def head_block(ref2d, col):
        return ref2d[:, _head_cols(col)]

    def per_kv(fn):
        return jnp.stack([fn(kv) for kv in range(kv_heads)])

    def grouped(ref2d, col0):
        return per_kv(lambda kv: jnp.concatenate(
            [head_block(ref2d, col0 + kv * group + g) for g in range(group)], axis=0))

    q = grouped(qkv_ref, q_col).astype(BF16)
    keys = (per_kv(lambda kv: ck_ref[0, :, _head_cols(kv)].astype(BF16)),
            per_kv(lambda kv: head_block(qkv_ref, k_col + kv).astype(BF16)))
    vals = (per_kv(lambda kv: cv_ref[0, :, _head_cols(kv)].astype(BF16)),
            per_kv(lambda kv: head_block(qkv_ref, v_col + kv).astype(BF16)))
    bias = b_ref[:, 0].reshape(kv_heads, group * t, b_ref.shape[3])
    biases = (bias[:, :, 0:cache_len], bias[:, :, cache_len:cache_len + t])
    sink = None
    if has_sink:
        sink = per_kv(lambda kv: jnp.concatenate(
            [jnp.full((t, 1), sink_ref[kv * group + g], F32) for g in range(group)], axis=0))
    out = _softmax_pv(q, keys, vals, biases, sink, grouped(g_ref, 0)).astype(o_ref.dtype)
    for kv in range(kv_heads):
        for g in range(group):
            h = kv * group + g
            o_ref[:, h * HEAD_DIM:(h + 1) * HEAD_DIM] = out[kv, g * t:(g + 1) * t, :]


def _attn_step(qkv, cache_k, cache_v, gates, bias, sinks, *, batch, heads, group, q_col, k_col, v_col, g_blk):
    rows = qkv.shape[0]
    t = rows // batch
    cache_len, kvw = cache_k.shape[1], cache_k.shape[2]
    nvar, bq, nk = bias.shape[1:]
    assert t <= CHUNK and cache_len + t <= nk and t <= bq
    has_sink = sinks is not None
    sink_arr = sinks.astype(F32) * LOG2E if has_sink else jnp.zeros((heads,), F32)
    width = heads * HEAD_DIM
    return pl.pallas_call(
        functools.partial(_attn_step_kernel, heads=heads, group=group, cache_len=cache_len,
                          q_col=q_col, k_col=k_col, v_col=v_col, has_sink=has_sink),
        grid=(batch,),
        in_specs=[pl.BlockSpec(memory_space=pltpu.SMEM),
                  pl.BlockSpec((t, qkv.shape[1]), lambda b: (b, 0)),
                  pl.BlockSpec((1, cache_len, kvw), lambda b: (b, 0, 0)),
                  pl.BlockSpec((1, cache_len, kvw), lambda b: (b, 0, 0)),
                  pl.BlockSpec((t, width), lambda b: (b, g_blk)),
                  pl.BlockSpec((heads, 1, t, nk), lambda b: (0, nvar - 1, 0, 0))],
        out_specs=pl.BlockSpec((t, width), lambda b: (b, 0)),
        out_shape=jax.ShapeDtypeStruct((rows, width), BF16),
        compiler_params=_params("parallel"),
        name="attn_step",
    )(sink_arr, qkv, cache_k, cache_v, gates, bias)


CONV_PAD = SUBLANES
CONV_STRIP = LANES


def _causal_dwconv_strip(pad_s, w_ref, s0, rows):
    cols = slice(s0, s0 + CONV_STRIP)
    xs = pad_s[:, cols]
    acc = w_ref[0:1, cols] * xs
    for k in range(1, w_ref.shape[0]):
        acc = w_ref[k:k + 1, cols] * xs + pltpu.roll(acc, 1, 0)
    return acc[CONV_PAD:CONV_PAD + rows]


def _ssd_kernel(xbc_ref, z_ref, dtr_ref, g_ref, b_ref, c2_ref, h2_ref, conv0_ref, dconv0_ref, h0_ref, cw_ref, cb_ref,
                dw_ref, dtb_ref, alog_ref, dsk_ref, cng_ref, exp_ref,
                y_ref, yd_ref, hout_ref, convout_ref, dconvout_ref, h_s, xpad_s, upad_s, act_s, gated_s, *, rows):
    c = pl.program_id(1)
    q = SSD_Q
    width = y_ref.shape[2]
    gw = width // C_GROUPS
    pad0 = CONV_PAD

    @pl.when(c == 0)
    def _():
        h_s[...] = h0_ref[0]
        xpad_s[0:pad0, :] = conv0_ref[0]
        upad_s[0:pad0, :] = dconv0_ref[0]

    upad_s[pad0:pad0 + rows, :] = c2_ref[0] * h2_ref[0]
    for s0 in range(0, upad_s.shape[1], CONV_STRIP):
        cols = slice(s0, s0 + CONV_STRIP)
        v = _causal_dwconv_strip(upad_s, dw_ref, s0, rows)
        yd_ref[0, :, cols] = (b_ref[0, :, cols] * v * _silu(g_ref[0, :, cols])).astype(yd_ref.dtype)
    dconvout_ref[0] = upad_s[rows:rows + pad0, :]
    upad_s[0:pad0, :] = upad_s[rows:rows + pad0, :]

    xpad_s[pad0:pad0 + rows, :] = xbc_ref[0]
    for s0 in range(0, xpad_s.shape[1], CONV_STRIP):
        cols = slice(s0, s0 + CONV_STRIP)
        act_s[0:rows, cols] = _silu(_causal_dwconv_strip(xpad_s, cw_ref, s0, rows) + cb_ref[:, cols])
    if rows < q:
        act_s[rows:q, :] = jnp.zeros((q - rows, act_s.shape[1]), F32)
    convout_ref[0] = xpad_s[rows:rows + pad0, :]
    xpad_s[0:pad0, :] = xpad_s[rows:rows + pad0, :]

    row = lax.broadcasted_iota(jnp.int32, (q, LANES), 0)
    lane = lax.broadcasted_iota(jnp.int32, (q, LANES), 1)
    causal = row >= (lane & (C_HEAD_DIM - 1))
    low_half = lane < C_HEAD_DIM
    diag_g = (lax.broadcasted_iota(jnp.int32, (q, gw), 0)
              == (lax.broadcasted_iota(jnp.int32, (q, gw), 1) & (C_HEAD_DIM - 1)))
    tril = (lax.broadcasted_iota(jnp.int32, (q, q), 0) >= lax.broadcasted_iota(jnp.int32, (q, q), 1)).astype(BF16)
    tril3 = jnp.concatenate([tril] * 3, axis=1)
    neg_a = -jnp.exp(alog_ref[...]) * LOG2E
    valid = min(rows, q)

    def pad_rows(v):
        return v if valid == q else jnp.concatenate([v, jnp.zeros((q - valid, v.shape[1]), F32)], axis=0)

    for r0 in range(0, rows, q):
        dt = jnp.where(row < valid, jax.nn.softplus(pad_rows(dtr_ref[0, r0:r0 + valid, :]) + dtb_ref[...]), 0.0)
        cum = jnp.dot(tril3, jnp.concatenate(_split3(dt * neg_a), axis=0), preferred_element_type=F32)
        spread = jnp.concatenate(_split3(jnp.concatenate([cum, dt], axis=0)), axis=1)
        ssq = jnp.zeros((q, 1), F32)
        for g in range(C_GROUPS):
            cols = slice(g * gw, (g + 1) * gw)
            ex = jnp.dot(spread, exp_ref[:, cols], preferred_element_type=F32)
            cexp, dtx = ex[0:q], ex[q:2 * q]
            x_g = act_s[r0:r0 + q, cols]
            b_g = act_s[r0:r0 + q, width + g * C_STATE:width + (g + 1) * C_STATE].astype(BF16)
            c_g = act_s[r0:r0 + q, width + (C_GROUPS + g) * C_STATE:width + (C_GROUPS + g + 1) * C_STATE].astype(BF16)
            xdt = x_g * dtx
            cum_row = jnp.sum(jnp.where(diag_g, cexp, 0.0), axis=0, keepdims=True)
            cum_last = cexp[q - 1:q, :]
            scores = lax.dot_general(c_g, jnp.concatenate([b_g, b_g], axis=0), (((1,), (1,)), ((), ())),
                                     preferred_element_type=F32)
            h_g = h_s[:, cols]
            y_off = jnp.dot(c_g, h_g.astype(BF16), preferred_element_type=F32) * jnp.exp2(cexp)
            y_diag = []
            for j in range(gw // LANES):
                pc = slice(j * LANES, (j + 1) * LANES)
                seg = cexp[:, pc] - cum_row[:, pc]
                lmat = jnp.exp2(jnp.where(causal, seg, NEG_INF))
                xp = xdt[:, pc]
                blockdiag = jnp.concatenate([jnp.where(low_half, xp, 0.0), jnp.where(low_half, 0.0, xp)], axis=0)
                y_diag.append(jnp.dot((scores * lmat).astype(BF16), blockdiag.astype(BF16),
                                      preferred_element_type=F32))
            y = jnp.concatenate(y_diag, axis=1) + y_off + dsk_ref[:, cols] * x_g
            xw = (xdt * jnp.exp2(cum_last - cexp)).astype(BF16)
            h_s[:, cols] = h_g * jnp.exp2(cum_last) + lax.dot_general(
                b_g, xw, (((0,), (0,)), ((), ())), preferred_element_type=F32)
            gated = y * _silu(pad_rows(z_ref[0, r0:r0 + valid, cols]))
            gated_s[:, cols] = gated
            ssq = ssq + jnp.sum(gated * gated, axis=-1, keepdims=True)
        inv = lax.rsqrt(ssq * (1.0 / width) + RMS_EPS)
        y_ref[0, r0:r0 + valid, :] = ((gated_s[...] * inv)[0:valid] * cng_ref[...]).astype(y_ref.dtype)

    @pl.when(c == pl.num_programs(1) - 1)
    def _():
        hout_ref[0] = h_s[...]


def _ssd(xbc, z, dtr, gbch, states, p, *, rows):
    b, t, conv_dim = xbc.shape
    width = z.shape[2]
    d_width = gbch.shape[2] // 4
    n_heads = width // C_HEAD_DIM
    ssm_state, conv_c_state, conv_d_state = states
    assert t % rows == 0 and rows % 16 == 0 and (rows <= SSD_Q or rows % SSD_Q == 0) and n_heads <= LANES
    assert conv_dim % CONV_STRIP == 0 and d_width % CONV_STRIP == 0
    pad_lanes = lambda v: jnp.pad(v.astype(F32), (0, LANES - v.shape[0])).reshape(1, LANES)
    pad_hist = lambda s: jnp.pad(s.astype(F32), ((0, 0), (CONV_PAD - s.shape[1], 0), (0, 0)))
    h0t = jnp.swapaxes(ssm_state.astype(F32).reshape(b, width, C_STATE), 1, 2)
    expand = (jnp.arange(LANES)[:, None] == (jnp.arange(width)[None, :] // C_HEAD_DIM)).astype(BF16)
    expand3 = jnp.concatenate([expand] * 3, axis=0)
    full = lambda shape: pl.BlockSpec(shape, lambda bi, ci: (0,) * len(shape))
    per_b = lambda shape: pl.BlockSpec(shape, lambda bi, ci: (bi,) + (0,) * (len(shape) - 1))
    blk = lambda w, k=0: pl.BlockSpec((1, rows, w), lambda bi, ci: (bi, ci, k))
    y, yd, h_out, conv_out, dconv_out = pl.pallas_call(
        functools.partial(_ssd_kernel, rows=rows),
        grid=(b, t // rows),
        in_specs=[blk(conv_dim), blk(width), blk(LANES),
                  blk(d_width, 0), blk(d_width, 1), blk(d_width, 2), blk(d_width, 3),
                  per_b((1, CONV_PAD, conv_dim)), per_b((1, CONV_PAD, d_width)), per_b((1, C_STATE, width)),
                  full((C_CONV_W, conv_dim)), full((1, conv_dim)), full((D_CONV_W, d_width)),
                  full((1, LANES)), full((1, LANES)), full((1, width)), full((1, width)),
                  full((3 * LANES, width))],
        out_specs=[blk(width), blk(d_width), per_b((1, C_STATE, width)),
                   per_b((1, CONV_PAD, conv_dim)), per_b((1, CONV_PAD, d_width))],
        out_shape=[jax.ShapeDtypeStruct((b, t, width), BF16),
                   jax.ShapeDtypeStruct((b, t, d_width), BF16),
                   jax.ShapeDtypeStruct((b, C_STATE, width), F32),
                   jax.ShapeDtypeStruct((b, CONV_PAD, conv_dim), F32),
                   jax.ShapeDtypeStruct((b, CONV_PAD, d_width), F32)],
        scratch_shapes=[pltpu.VMEM((C_STATE, width), F32),
                        pltpu.VMEM((CONV_PAD + rows, conv_dim), F32),
                        pltpu.VMEM((CONV_PAD + rows, d_width), F32),
                        pltpu.VMEM((max(rows, SSD_Q), conv_dim), F32),
                        pltpu.VMEM((SSD_Q, width), F32)],
        compiler_params=_params("parallel", "arbitrary"),
        name="ssd",
    )(xbc, z, dtr, gbch, gbch, gbch, gbch, pad_hist(conv_c_state), pad_hist(conv_d_state), h0t,
      p["conv_c_w"].astype(F32), p["conv_c_b"].astype(F32).reshape(1, conv_dim), p["conv_d_w"].astype(F32),
      pad_lanes(p["dt_bias"]), pad_lanes(p["a_log"]),
      jnp.repeat(p["d_skip"].astype(F32), C_HEAD_DIM).reshape(1, width),
      p["c_norm"].astype(F32).reshape(1, width), expand3)
    new_state = jnp.swapaxes(h_out, 1, 2).reshape(b, n_heads, C_HEAD_DIM, C_STATE)
    return (y, yd, new_state, conv_out[:, CONV_PAD - (C_CONV_W - 1):, :],
            dconv_out[:, CONV_PAD - (D_CONV_W - 1):, :])


TN_WIDE = 1024
TN_NARROW = 512


def _even_weights(w_in, w_out):
    aw = w_out.shape[0] // 2
    bkv = (w_in.shape[1] - 6 * aw) // 2
    assert aw % TN_WIDE == 0 and (2 * bkv) % TN_NARROW == 0 and aw % TN_NARROW == 0
    return dict(w_in=w_in.astype(BF16), w_out=w_out.astype(BF16), aw=aw, bkv=bkv)


def _in_proj_even(hn, w, kv_dtype):
    aw, wb = w["aw"], w["w_in"]
    qkv_a = _matmul([hn], [(wb, 0)], kv_dtype, n=3 * aw, tn=TN_WIDE, name="in_proj_qkv_a")
    nq = aw // TN_NARROW
    qkv_b = _matmul([hn], [(wb, 0)], kv_dtype, n=aw + 2 * w["bkv"], tn=TN_NARROW, name="in_proj_qkv_b",
                    wcol=lambda j: jnp.where(j < nq, 4 * nq + j, 6 * nq + j - nq))
    ng = aw // TN_WIDE
    gates = _matmul([hn], [(wb, 0)], F32, n=2 * aw, tn=TN_WIDE, name="in_proj_gates",
                    wcol=lambda j: jnp.where(j < ng, 3 * ng + j, 5 * ng + j - ng))
    return qkv_a, qkv_b, gates


def _attn_layer(x2d, hn, w, bias_a, bias_b, sinks, caches, batch):
    aw, bkv = w["aw"], w["bkv"]
    heads = aw // HEAD_DIM
    qkv_a, qkv_b, gates = _in_proj_even(hn, w, BF16 if caches is None else F32)
    kvh_b = bkv // HEAD_DIM
    if caches is None:
        oa = _attn_prompt(qkv_a, gates, bias_a, None, heads=heads, group=1, n_back=A_BACK_CHUNKS,
                          q_col=0, k_col=heads, v_col=2 * heads, g_col=0)
        ob = _attn_prompt(qkv_b, gates, bias_b, sinks, heads=heads, group=heads // kvh_b, n_back=B_BACK_CHUNKS,
                          q_col=0, k_col=heads, v_col=heads + kvh_b, g_col=heads)
    else:
        ck_a, cv_a, ck_b, cv_b = caches
        flat = lambda c: lax.optimization_barrier(c.astype(BF16)).reshape(c.shape[0], c.shape[1], -1)
        oa = _attn_step(qkv_a, flat(ck_a), flat(cv_a), gates, bias_a, None, batch=batch, heads=heads, group=1,
                        q_col=0, k_col=heads, v_col=2 * heads, g_blk=0)
        ob = _attn_step(qkv_b, flat(ck_b), flat(cv_b), gates, bias_b, sinks, batch=batch, heads=heads,
                        group=heads // kvh_b, q_col=0, k_col=heads, v_col=heads + kvh_b, g_blk=1)
    x_new = _matmul([oa, ob], [(w["w_out"], 0), (w["w_out"], 1)], F32, res=x2d, tn=TN_WIDE, name="out_proj_even")
    return x_new, qkv_a, qkv_b


def _odd_weights(w_in, w_out, c_width, conv_dim, c_heads):
    o1 = c_width + conv_dim
    assert c_width % TN_WIDE == 0 and conv_dim % TN_WIDE == 0
    wb = w_in.astype(BF16)
    return dict(
        w_in=wb,
        dt=jnp.pad(wb[:, o1:o1 + c_heads], ((0, 0), (0, LANES - c_heads))),
        gbch=wb[:, o1 + c_heads:],
        w_out=w_out.astype(BF16))


def _ssm_conv_layer(x2d, hn, w, p, states, batch, rows):
    c_width = p["c_norm"].shape[0]
    conv_dim = p["conv_c_w"].shape[1]
    d_width = p["conv_d_w"].shape[1]
    t = x2d.shape[0] // batch
    nz = c_width // TN_WIDE
    z = _matmul([hn], [(w["w_in"], 0)], F32, n=c_width, tn=TN_WIDE, name="in_proj_z").reshape(batch, t, c_width)
    xbc = _matmul([hn], [(w["w_in"], 0)], F32, n=conv_dim, tn=TN_WIDE, wcol=lambda j: nz + j,
                  name="in_proj_xbc").reshape(batch, t, conv_dim)
    dtr = _matmul([hn], [(w["dt"], 0)], F32, name="in_proj_dt").reshape(batch, t, LANES)
    gbch = _matmul([hn], [(w["gbch"], 0)], F32, tn=TN_WIDE, name="in_proj_gbch").reshape(batch, t, -1)
    yc, yd, new_ssm, new_cc, new_dc = _ssd(xbc, z, dtr, gbch, states, p, rows=rows)
    assert c_width % d_width == 0
    x_new = _matmul([yc.reshape(batch * t, c_width), yd.reshape(batch * t, d_width)],
                    [(w["w_out"], 0), (w["w_out"], c_width // d_width)], F32,
                    res=x2d, tn=TN_NARROW, name="out_proj_odd")
    return x_new, new_ssm, new_cc, new_dc


def kernel(x_prompt, x_sample, cache_a_k, cache_a_v, cache_b_k, cache_b_v, state_c_ssm, state_c_conv, state_d_conv, norm_even, w_in_even, w_out_even, rel_bias_a, sinks_b, norm_odd, w_in_odd, w_out_odd, conv_c_w, conv_c_b, dt_bias, a_log, d_skip, c_norm, conv_d_w, final_norm):
    bp, tp, d = x_prompt.shape
    bs, ts, _ = x_sample.shape
    assert bp == 1, "the prompt attention kernel handles one prompt sequence"
    dt_out = x_prompt.dtype
    depth = norm_even.shape[0] + norm_odd.shape[0]
    xp = x_prompt.astype(F32).reshape(bp * tp, d)
    xs = x_sample.astype(F32).reshape(bs * ts, d)
    keep_a = min(A_BACK_CHUNKS * CHUNK, tp)
    keep_b = min(B_BACK_CHUNKS * CHUNK, tp)
    outs = {k: [] for k in ("p_ak", "p_av", "p_bk", "p_bv", "p_ssm", "p_cc", "p_dc",
                            "s_ak", "s_av", "s_bk", "s_bv", "s_ssm", "s_cc", "s_dc")}
    for layer in range(depth):
        i = layer // 2
        if layer % 2 == 0:
            w = _even_weights(w_in_even[i], w_out_even[i])
            aw, bkv = w["aw"], w["bkv"]
            heads = aw // HEAD_DIM
            bias_a = _bias_tiles_a(rel_bias_a[i], ATTN_TILE_ROWS)
            bias_b = _bias_tiles_b(heads, ATTN_TILE_ROWS)
            hp = _rmsnorm(xp, norm_even[i], BF16)
            hs = _rmsnorm(xs, norm_even[i], BF16)
            xp_new, _, _ = _attn_layer(xp, hp, w, bias_a, bias_b, sinks_b[i], None, bp)
            kv_a = _matmul([hp[tp - keep_a:]], [(w["w_in"], 0)], F32, n=2 * aw, tn=TN_WIDE,
                           wcol=lambda j: aw // TN_WIDE + j, name="cache_kv_a")
            kv_b = _matmul([hp[tp - keep_b:]], [(w["w_in"], 0)], F32, n=2 * bkv, tn=TN_NARROW,
                           wcol=lambda j: 6 * aw // TN_NARROW + j, name="cache_kv_b")
            outs["p_ak"].append(kv_a[:, :aw].reshape(bp, keep_a, heads, HEAD_DIM))
            outs["p_av"].append(kv_a[:, aw:].reshape(bp, keep_a, heads, HEAD_DIM))
            outs["p_bk"].append(kv_b[:, :bkv].reshape(bp, keep_b, bkv // HEAD_DIM, HEAD_DIM))
            outs["p_bv"].append(kv_b[:, bkv:].reshape(bp, keep_b, bkv // HEAD_DIM, HEAD_DIM))
            xs_new, qkv_a, qkv_b = _attn_layer(xs, hs, w, bias_a, bias_b, sinks_b[i],
                                               (cache_a_k[i], cache_a_v[i], cache_b_k[i], cache_b_v[i]), bs)
            outs["s_ak"].append(qkv_a[:, aw:2 * aw].reshape(bs, ts, heads, HEAD_DIM))
            outs["s_av"].append(qkv_a[:, 2 * aw:].reshape(bs, ts, heads, HEAD_DIM))
            outs["s_bk"].append(qkv_b[:, aw:aw + bkv].reshape(bs, ts, bkv // HEAD_DIM, HEAD_DIM))
            outs["s_bv"].append(qkv_b[:, aw + bkv:].reshape(bs, ts, bkv // HEAD_DIM, HEAD_DIM))
            xp, xs = xp_new, xs_new
        else:
            c_width = c_norm.shape[1]
            conv_dim = conv_c_w.shape[2]
            c_heads = dt_bias.shape[1]
            d_width = conv_d_w.shape[2]
            w = _odd_weights(w_in_odd[i], w_out_odd[i], c_width, conv_dim, c_heads)
            p = dict(conv_c_w=conv_c_w[i], conv_c_b=conv_c_b[i], dt_bias=dt_bias[i], a_log=a_log[i],
                     d_skip=d_skip[i], c_norm=c_norm[i], conv_d_w=conv_d_w[i])
            zero = (jnp.zeros((bp, c_heads, C_HEAD_DIM, C_STATE), F32),
                    jnp.zeros((bp, C_CONV_W - 1, conv_dim), F32),
                    jnp.zeros((bp, D_CONV_W - 1, d_width), F32))
            hp = _rmsnorm(xp, norm_odd[i], BF16)
            hs = _rmsnorm(xs, norm_odd[i], BF16)
            xp, ssm, cc, dc = _ssm_conv_layer(xp, hp, w, p, zero, bp, min(SSD_ROWS_PER_STEP, tp))
            xs, sssm, scc, sdc = _ssm_conv_layer(xs, hs, w, p, (state_c_ssm[i], state_c_conv[i], state_d_conv[i]),
                                                 bs, ts)
            outs["p_ssm"].append(ssm.astype(dt_out)); outs["p_cc"].append(cc.astype(dt_out))
            outs["p_dc"].append(dc.astype(dt_out))
            outs["s_ssm"].append(sssm.astype(dt_out)); outs["s_cc"].append(scc.astype(dt_out))
            outs["s_dc"].append(sdc.astype(dt_out))
    y_prompt = _rmsnorm(xp, final_norm, dt_out).reshape(bp, tp, d)
    y_sample = _rmsnorm(xs, final_norm, dt_out).reshape(bs, ts, d)
    st = lambda k: jnp.stack(outs[k])
    return (y_prompt, y_sample, st("p_ak"), st("p_av"), st("p_bk"), st("p_bv"), st("p_ssm"), st("p_cc"), st("p_dc"),
            st("s_ak"), st("s_av"), st("s_bk"), st("s_bv"), st("s_ssm"), st("s_cc"), st("s_dc"))
```

```python
import functools

import jax
import jax.numpy as jnp
from jax import lax
from jax.experimental import pallas as pl
from jax.experimental.pallas import tpu as pltpu

F32 = jnp.float32
BF16 = jnp.bfloat16

CHUNK = 64
HEAD_DIM = 128
A_BACK_CHUNKS = 8
B_BACK_CHUNKS = 2
B_GROUP = 8
REL_CLIP = 128
C_HEAD_DIM = 64
C_GROUPS = 8
C_STATE = 128
C_CONV_W = 4
D_CONV_W = 3
RMS_EPS = 1e-6
NEG_INF = -1e30
ATTN_SCALE = HEAD_DIM ** -0.5
LOG2E = 1.4426950408889634

V7X_VMEM_BYTES = 64 * 1024 * 1024
VMEM_LIMIT_BYTES = V7X_VMEM_BYTES - 8 * 1024 * 1024
LANES = 128
SUBLANES = 8

ATTN_TILE_ROWS = 128
ATTN_ROWS_PER_STEP = 512
ATTN_HEADS_PER_STEP = 8
ATTN_HEADS_PER_UNIT = 1
SSD_Q = CHUNK
SSD_ROWS_PER_STEP = 2 * SSD_Q


def _params(*sem):
    return pltpu.CompilerParams(dimension_semantics=sem, vmem_limit_bytes=VMEM_LIMIT_BYTES)


def _silu(x):
    h = 0.5 * x
    return h + h * jnp.tanh(h)


def _split3(x):
    hi = x.astype(BF16)
    r1 = x - hi.astype(F32)
    mid = r1.astype(BF16)
    lo = (r1 - mid.astype(F32)).astype(BF16)
    return hi, mid, lo


def _rmsnorm_kernel(x_ref, g_ref, o_ref):
    x = x_ref[...]
    y = x * lax.rsqrt(jnp.mean(x * x, axis=-1, keepdims=True) + RMS_EPS)
    o_ref[...] = (y * g_ref[...]).astype(o_ref.dtype)


def _rmsnorm(x2d, gain, out_dtype, rows=512):
    m, d = x2d.shape
    rows = min(rows, m)
    assert m % rows == 0, (m, rows)
    return pl.pallas_call(
        _rmsnorm_kernel,
        grid=(m // rows,),
        in_specs=[pl.BlockSpec((rows, d), lambda i: (i, 0)), pl.BlockSpec((1, d), lambda i: (0, 0))],
        out_specs=pl.BlockSpec((rows, d), lambda i: (i, 0)),
        out_shape=jax.ShapeDtypeStruct((m, d), out_dtype),
        compiler_params=_params("parallel"),
        name="rmsnorm",
    )(x2d, gain.reshape(1, d).astype(F32))


def _matmul_kernel(*refs, n_lhs, has_res):
    o_ref = refs[-1]
    acc = None
    for i in range(n_lhs):
        part = jnp.dot(refs[i][...], refs[n_lhs + i][...], preferred_element_type=F32)
        acc = part if acc is None else acc + part
    if has_res:
        acc = acc + refs[2 * n_lhs][...]
    o_ref[...] = acc.astype(o_ref.dtype)


def _matmul(lhs_list, w_list, out_dtype, *, n=None, wcol=None, res=None, tm=1024, tn=1024, name="matmul"):
    m = lhs_list[0].shape[0]
    n = w_list[0][0].shape[1] if n is None else n
    wcol = (lambda j: j) if wcol is None else wcol
    tm, tn = min(tm, m), min(tn, n)
    assert m % tm == 0 and n % tn == 0, (m, n, tm, tn)
    in_specs = [pl.BlockSpec((tm, a.shape[1]), lambda i, j: (i, 0)) for a in lhs_list]
    in_specs += [pl.BlockSpec((a.shape[1], tn), lambda i, j, kblk=kblk: (kblk, wcol(j)))
                 for a, (_, kblk) in zip(lhs_list, w_list)]
    args = list(lhs_list) + [w for w, _ in w_list]
    if res is not None:
        in_specs.append(pl.BlockSpec((tm, tn), lambda i, j: (i, j)))
        args.append(res)
    return pl.pallas_call(
        functools.partial(_matmul_kernel, n_lhs=len(lhs_list), has_res=res is not None),
        grid=(m // tm, n // tn),
        in_specs=in_specs,
        out_specs=pl.BlockSpec((tm, tn), lambda i, j: (i, j)),
        out_shape=jax.ShapeDtypeStruct((m, n), out_dtype),
        compiler_params=_params("parallel", "arbitrary"),
        name=name,
    )(*args)


def _band_ok(i, j, var, bq, n_back):
    qc = i >> 6
    kc = j >> 6
    kpos = var * bq - n_back * CHUNK + j
    return (kc >= qc) & (kc <= qc + n_back) & (kpos >= 0)


def _bias_a_kernel(tab_ref, o_ref, *, bq, nk, n_back):
    var = pl.program_id(1)
    for c in range(nk // LANES):
        i = lax.broadcasted_iota(jnp.int32, (bq, LANES), 0)
        j = lax.broadcasted_iota(jnp.int32, (bq, LANES), 1) + c * LANES
        idx = jnp.clip(i - j + n_back * CHUNK, -REL_CLIP, REL_CLIP) + REL_CLIP
        lane = idx & (LANES - 1)
        seg = idx >> 7
        val = jnp.zeros((bq, LANES), F32)
        for s in range(tab_ref.shape[1]):
            tab = jnp.broadcast_to(tab_ref[0, s:s + 1, :], (bq, LANES))
            val = jnp.where(seg == s, jnp.take_along_axis(tab, lane, axis=1), val)
        o_ref[0, 0, :, c * LANES:(c + 1) * LANES] = jnp.where(_band_ok(i, j, var, bq, n_back), val * LOG2E, NEG_INF)


def _bias_b_kernel(slope_ref, o_ref, *, bq, nk, n_back):
    h = pl.program_id(0)
    var = pl.program_id(1)
    i = lax.broadcasted_iota(jnp.int32, (bq, nk), 0)
    j = lax.broadcasted_iota(jnp.int32, (bq, nk), 1)
    dist = jnp.abs(i - j + n_back * CHUNK).astype(F32)
    val = (-slope_ref[h]) * dist
    o_ref[0, 0] = jnp.where(_band_ok(i, j, var, bq, n_back), val * LOG2E, NEG_INF)


def _tile_geometry(n_back, bq):
    back_rows = n_back * CHUNK
    return back_rows + bq, -(-back_rows // bq) + 1


def _window_blocks(n_back, rows):
    back_rows = n_back * CHUNK
    kb = min(rows, back_rows)
    assert rows % kb == 0 and back_rows % kb == 0
    return kb, tuple(range(-(back_rows // kb), rows // kb))


def _bias_tiles_a(rel_table, bq):
    heads, width = rel_table.shape
    nseg = -(-width // LANES)
    tab = jnp.pad(rel_table.astype(F32), ((0, 0), (0, nseg * LANES - width))).reshape(heads, nseg, LANES)
    nk, nvar = _tile_geometry(A_BACK_CHUNKS, bq)
    return pl.pallas_call(
        functools.partial(_bias_a_kernel, bq=bq, nk=nk, n_back=A_BACK_CHUNKS),
        grid=(heads, nvar),
        in_specs=[pl.BlockSpec((1, nseg, LANES), lambda h, v: (h, 0, 0))],
        out_specs=pl.BlockSpec((1, 1, bq, nk), lambda h, v: (h, v, 0, 0)),
        out_shape=jax.ShapeDtypeStruct((heads, nvar, bq, nk), F32),
        compiler_params=_params("parallel", "parallel"),
        name="bias_a",
    )(tab)


def _bias_tiles_b(heads, bq):
    slopes = 2.0 ** (-8.0 * jnp.arange(1, heads + 1, dtype=F32) / heads)
    nk, nvar = _tile_geometry(B_BACK_CHUNKS, bq)
    return pl.pallas_call(
        functools.partial(_bias_b_kernel, bq=bq, nk=nk, n_back=B_BACK_CHUNKS),
        grid=(heads, nvar),
        in_specs=[pl.BlockSpec(memory_space=pltpu.SMEM)],
        out_specs=pl.BlockSpec((1, 1, bq, nk), lambda h, v: (h, v, 0, 0)),
        out_shape=jax.ShapeDtypeStruct((heads, nvar, bq, nk), F32),
        compiler_params=_params("parallel", "parallel"),
        name="bias_b",
    )(slopes)


def _softmax_pv(q, keys, vals, biases, sink, gate):
    logits = [jnp.einsum("nqd,nkd->nqk", q, k, preferred_element_type=F32) * (ATTN_SCALE * LOG2E) + b
              for k, b in zip(keys, biases)]
    m = functools.reduce(jnp.maximum, [jnp.max(s, axis=-1, keepdims=True) for s in logits])
    if sink is not None:
        m = jnp.maximum(m, sink)
    probs = [jnp.exp2(s - m) for s in logits]
    denom = functools.reduce(jnp.add, [jnp.sum(p, axis=-1, keepdims=True) for p in probs])
    if sink is not None:
        denom = denom + jnp.exp2(sink - m)
    o = functools.reduce(jnp.add, [jnp.einsum("nqk,nkd->nqd", p.astype(BF16), v, preferred_element_type=F32)
                                   for p, v in zip(probs, vals)])
    return (o / denom) * _silu(gate)


def _head_cols(i):
    return slice(i * HEAD_DIM, (i + 1) * HEAD_DIM)


def _attn_prompt_kernel(*refs, nblk, bq, nk, ntile, hps, kvps, hpu, has_sink):
    sink_ref, q_ref = refs[0], refs[1]
    k_refs = refs[2:2 + nblk]
    v_refs = refs[2 + nblk:2 + 2 * nblk]
    g_ref = refs[2 + 2 * nblk]
    b_refs = refs[3 + 2 * nblk:3 + 2 * nblk + ntile]
    o_ref = refs[-1]
    group = hps // kvps
    head0 = pl.program_id(0) * hps
    units = [(kv, r, kv * group + c, hpu) for kv in range(kvps) for r in range(ntile) for c in range(0, group, hpu)]

    def window(blk_refs, kv):
        return jnp.concatenate([ref[:, _head_cols(kv)] for ref in blk_refs], axis=0).astype(BF16)

    def stacked(ref, r, h0, nh):
        return jnp.concatenate([ref[r * bq:(r + 1) * bq, _head_cols(h0 + g)] for g in range(nh)], axis=0)

    kwin = [window(k_refs, kv) for kv in range(kvps)]
    vwin = [window(v_refs, kv) for kv in range(kvps)]
    q = jnp.stack([stacked(q_ref, r, h0, nh) for _, r, h0, nh in units]).astype(BF16)
    k = jnp.stack([kwin[kv][r * bq:r * bq + nk] for kv, r, _, _ in units])
    v = jnp.stack([vwin[kv][r * bq:r * bq + nk] for kv, r, _, _ in units])
    bias = jnp.stack([b_refs[r][h0:h0 + nh, 0].reshape(nh * bq, nk) for _, r, h0, nh in units])
    gate = jnp.stack([stacked(g_ref, r, h0, nh) for _, r, h0, nh in units])
    sink = None
    if has_sink:
        sink = jnp.stack([jnp.concatenate([jnp.full((bq, 1), sink_ref[head0 + h0 + g], F32)
                                           for g in range(nh)], axis=0) for _, _, h0, nh in units])
    out = _softmax_pv(q, [k], [v], [bias], sink, gate).astype(o_ref.dtype)
    for n, (_, r, h0, nh) in enumerate(units):
        for g in range(nh):
            o_ref[r * bq:(r + 1) * bq, _head_cols(h0 + g)] = out[n, g * bq:(g + 1) * bq, :]


def _attn_prompt(qkv, gates, bias, sinks, *, heads, group, n_back, q_col, k_col, v_col, g_col):
    t = qkv.shape[0]
    rows, bq = ATTN_ROWS_PER_STEP, ATTN_TILE_ROWS
    ntile = rows // bq
    hps = ATTN_HEADS_PER_STEP
    assert hps % group == 0 and heads % hps == 0
    kvps = hps // group
    nk, nvar = _tile_geometry(n_back, bq)
    kb, offs = _window_blocks(n_back, rows)
    assert t % rows == 0 and bias.shape == (heads, nvar, bq, nk)
    assert q_col % hps == 0 and g_col % hps == 0 and k_col % kvps == 0 and v_col % kvps == 0
    has_sink = sinks is not None
    sink_arr = sinks.astype(F32) * LOG2E if has_sink else jnp.zeros((heads,), F32)

    def kv_spec(col, off):
        return pl.BlockSpec((kb, kvps * HEAD_DIM),
                            lambda hb, qi: (jnp.maximum(qi * (rows // kb) + off, 0), col // kvps + hb))

    def bias_spec(r):
        return pl.BlockSpec((hps, 1, bq, nk), lambda hb, qi: (hb, jnp.minimum(qi * ntile + r, nvar - 1), 0, 0))

    in_specs = [pl.BlockSpec(memory_space=pltpu.SMEM),
                pl.BlockSpec((rows, hps * HEAD_DIM), lambda hb, qi: (qi, q_col // hps + hb))]
    in_specs += [kv_spec(k_col, off) for off in offs]
    in_specs += [kv_spec(v_col, off) for off in offs]
    in_specs += [pl.BlockSpec((rows, hps * HEAD_DIM), lambda hb, qi: (qi, g_col // hps + hb))]
    in_specs += [bias_spec(r) for r in range(ntile)]
    return pl.pallas_call(
        functools.partial(_attn_prompt_kernel, nblk=len(offs), bq=bq, nk=nk, ntile=ntile, hps=hps, kvps=kvps,
                          hpu=min(group, ATTN_HEADS_PER_UNIT), has_sink=has_sink),
        grid=(heads // hps, t // rows),
        in_specs=in_specs,
        out_specs=pl.BlockSpec((rows, hps * HEAD_DIM), lambda hb, qi: (qi, hb)),
        out_shape=jax.ShapeDtypeStruct((t, heads * HEAD_DIM), BF16),
        compiler_params=_params("parallel", "arbitrary"),
        name="attn_prompt",
    )(sink_arr, qkv, *([qkv] * (2 * len(offs))), gates, *([bias] * ntile))


def _attn_step_kernel(sink_ref, qkv_ref, ck_ref, cv_ref, g_ref, b_ref, o_ref, *,
                      heads, group, cache_len, q_col, k_col, v_col, has_sink):
    t = qkv_ref.shape[0]
    kv_heads = heads // group

    def head_block(ref2d, col):
        return ref2d[:, _head_cols(col)]

    def per_kv(fn):
        return jnp.stack([fn(kv) for kv in range(kv_heads)])

    def grouped(ref2d, col0):
        return per_kv(lambda kv: jnp.concatenate(
            [head_block(ref2d, col0 + kv * group + g) for g in range(group)], axis=0))

    def cached(ref, kv):
        return ref[0, pl.ds(kv, cache_len, stride=kv_heads), :].astype(BF16)

    q = grouped(qkv_ref, q_col).astype(BF16)
    keys = (per_kv(lambda kv: cached(ck_ref, kv)),
            per_kv(lambda kv: head_block(qkv_ref, k_col + kv).astype(BF16)))
    vals = (per_kv(lambda kv: cached(cv_ref, kv)),
            per_kv(lambda kv: head_block(qkv_ref, v_col + kv).astype(BF16)))
    bias = b_ref[:, 0].reshape(kv_heads, group * t, b_ref.shape[3])
    biases = (bias[:, :, 0:cache_len], bias[:, :, cache_len:cache_len + t])
    sink = None
    if has_sink:
        sink = per_kv(lambda kv: jnp.concatenate(
            [jnp.full((t, 1), sink_ref[kv * group + g], F32) for g in range(group)], axis=0))
    out = _softmax_pv(q, keys, vals, biases, sink, grouped(g_ref, 0)).astype(o_ref.dtype)
    for kv in range(kv_heads):
        for g in range(group):
            h = kv * group + g
            o_ref[:, h * HEAD_DIM:(h + 1) * HEAD_DIM] = out[kv, g * t:(g + 1) * t, :]


def _attn_step(qkv, cache_k, cache_v, gates, bias, sinks, *, batch, heads, group, q_col, k_col, v_col, g_blk):
    rows = qkv.shape[0]
    t = rows // batch
    cache_rows = cache_k.shape[1]
    cache_len = cache_rows // (heads // group)
    nvar, bq, nk = bias.shape[1:]
    assert t <= CHUNK and cache_len + t <= nk and t <= bq
    has_sink = sinks is not None
    sink_arr = sinks.astype(F32) * LOG2E if has_sink else jnp.zeros((heads,), F32)
    width = heads * HEAD_DIM
    return pl.pallas_call(
        functools.partial(_attn_step_kernel, heads=heads, group=group, cache_len=cache_len,
                          q_col=q_col, k_col=k_col, v_col=v_col, has_sink=has_sink),
        grid=(batch,),
        in_specs=[pl.BlockSpec(memory_space=pltpu.SMEM),
                  pl.BlockSpec((t, qkv.shape[1]), lambda b: (b, 0)),
                  pl.BlockSpec((1, cache_rows, HEAD_DIM), lambda b: (b, 0, 0)),
                  pl.BlockSpec((1, cache_rows, HEAD_DIM), lambda b: (b, 0, 0)),
                  pl.BlockSpec((t, width), lambda b: (b, g_blk)),
                  pl.BlockSpec((heads, 1, t, nk), lambda b: (0, nvar - 1, 0, 0))],
        out_specs=pl.BlockSpec((t, width), lambda b: (b, 0)),
        out_shape=jax.ShapeDtypeStruct((rows, width), BF16),
        compiler_params=_params("parallel"),
        name="attn_step",
    )(sink_arr, qkv, cache_k, cache_v, gates, bias)


CONV_PAD = SUBLANES
CONV_STRIP = LANES


def _causal_dwconv_strip(pad_s, w_ref, s0, rows):
    cols = slice(s0, s0 + CONV_STRIP)
    xs = pad_s[:, cols]
    acc = w_ref[0:1, cols] * xs
    for k in range(1, w_ref.shape[0]):
        acc = w_ref[k:k + 1, cols] * xs + pltpu.roll(acc, 1, 0)
    return acc[CONV_PAD:CONV_PAD + rows]


def _ssd_kernel(xbc_ref, z_ref, dtr_ref, g_ref, b_ref, c2_ref, h2_ref, conv0_ref, dconv0_ref, h0_ref, cw_ref, cb_ref,
                dw_ref, dtb_ref, alog_ref, dsk_ref, cng_ref, exp_ref,
                y_ref, yd_ref, hout_ref, convout_ref, dconvout_ref, h_s, xpad_s, upad_s, act_s, gated_s, *, rows):
    c = pl.program_id(1)
    q = SSD_Q
    width = y_ref.shape[2]
    gw = width // C_GROUPS
    pad0 = CONV_PAD

    @pl.when(c == 0)
    def _():
        h_s[...] = h0_ref[0]
        xpad_s[0:pad0, :] = conv0_ref[0]
        upad_s[0:pad0, :] = dconv0_ref[0]

    upad_s[pad0:pad0 + rows, :] = c2_ref[0] * h2_ref[0]
    for s0 in range(0, upad_s.shape[1], CONV_STRIP):
        cols = slice(s0, s0 + CONV_STRIP)
        v = _causal_dwconv_strip(upad_s, dw_ref, s0, rows)
        yd_ref[0, :, cols] = (b_ref[0, :, cols] * v * _silu(g_ref[0, :, cols])).astype(yd_ref.dtype)
    dconvout_ref[0] = upad_s[rows:rows + pad0, :]
    upad_s[0:pad0, :] = upad_s[rows:rows + pad0, :]

    xpad_s[pad0:pad0 + rows, :] = xbc_ref[0]
    for s0 in range(0, xpad_s.shape[1], CONV_STRIP):
        cols = slice(s0, s0 + CONV_STRIP)
        act_s[0:rows, cols] = _silu(_causal_dwconv_strip(xpad_s, cw_ref, s0, rows) + cb_ref[:, cols])
    if rows < q:
        act_s[rows:q, :] = jnp.zeros((q - rows, act_s.shape[1]), F32)
    convout_ref[0] = xpad_s[rows:rows + pad0, :]
    xpad_s[0:pad0, :] = xpad_s[rows:rows + pad0, :]

    row = lax.broadcasted_iota(jnp.int32, (q, LANES), 0)
    lane = lax.broadcasted_iota(jnp.int32, (q, LANES), 1)
    causal = row >= (lane & (C_HEAD_DIM - 1))
    low_half = lane < C_HEAD_DIM
    diag_g = (lax.broadcasted_iota(jnp.int32, (q, gw), 0)
              == (lax.broadcasted_iota(jnp.int32, (q, gw), 1) & (C_HEAD_DIM - 1)))
    tril = (lax.broadcasted_iota(jnp.int32, (q, q), 0) >= lax.broadcasted_iota(jnp.int32, (q, q), 1)).astype(BF16)
    tril3 = jnp.concatenate([tril] * 3, axis=1)
    neg_a = -jnp.exp(alog_ref[...]) * LOG2E
    valid = min(rows, q)

    def pad_rows(v):
        return v if valid == q else jnp.concatenate([v, jnp.zeros((q - valid, v.shape[1]), F32)], axis=0)

    for r0 in range(0, rows, q):
        dt = jnp.where(row < valid, jax.nn.softplus(pad_rows(dtr_ref[0, r0:r0 + valid, :]) + dtb_ref[...]), 0.0)
        cum = jnp.dot(tril3, jnp.concatenate(_split3(dt * neg_a), axis=0), preferred_element_type=F32)
        spread = jnp.concatenate(_split3(jnp.concatenate([cum, dt], axis=0)), axis=1)
        ssq = jnp.zeros((q, 1), F32)
        for g in range(C_GROUPS):
            cols = slice(g * gw, (g + 1) * gw)
            ex = jnp.dot(spread, exp_ref[:, cols], preferred_element_type=F32)
            cexp, dtx = ex[0:q], ex[q:2 * q]
            x_g = act_s[r0:r0 + q, cols]
            b_g = act_s[r0:r0 + q, width + g * C_STATE:width + (g + 1) * C_STATE].astype(BF16)
            c_g = act_s[r0:r0 + q, width + (C_GROUPS + g) * C_STATE:width + (C_GROUPS + g + 1) * C_STATE].astype(BF16)
            xdt = x_g * dtx
            cum_row = jnp.sum(jnp.where(diag_g, cexp, 0.0), axis=0, keepdims=True)
            cum_last = cexp[q - 1:q, :]
            scores = lax.dot_general(c_g, jnp.concatenate([b_g, b_g], axis=0), (((1,), (1,)), ((), ())),
                                     preferred_element_type=F32)
            h_g = h_s[:, cols]
            y_off = jnp.dot(c_g, h_g.astype(BF16), preferred_element_type=F32) * jnp.exp2(cexp)
            y_diag = []
            for j in range(gw // LANES):
                pc = slice(j * LANES, (j + 1) * LANES)
                seg = cexp[:, pc] - cum_row[:, pc]
                lmat = jnp.exp2(jnp.where(causal, seg, NEG_INF))
                xp = xdt[:, pc]
                blockdiag = jnp.concatenate([jnp.where(low_half, xp, 0.0), jnp.where(low_half, 0.0, xp)], axis=0)
                y_diag.append(jnp.dot((scores * lmat).astype(BF16), blockdiag.astype(BF16),
                                      preferred_element_type=F32))
            y = jnp.concatenate(y_diag, axis=1) + y_off + dsk_ref[:, cols] * x_g
            xw = (xdt * jnp.exp2(cum_last - cexp)).astype(BF16)
            h_s[:, cols] = h_g * jnp.exp2(cum_last) + lax.dot_general(
                b_g, xw, (((0,), (0,)), ((), ())), preferred_element_type=F32)
            gated = y * _silu(pad_rows(z_ref[0, r0:r0 + valid, cols]))
            gated_s[:, cols] = gated
            ssq = ssq + jnp.sum(gated * gated, axis=-1, keepdims=True)
        inv = lax.rsqrt(ssq * (1.0 / width) + RMS_EPS)
        y_ref[0, r0:r0 + valid, :] = ((gated_s[...] * inv)[0:valid] * cng_ref[...]).astype(y_ref.dtype)

    @pl.when(c == pl.num_programs(1) - 1)
    def _():
        hout_ref[0] = h_s[...]


def _ssd(xbc, z, dtr, gbch, states, p, *, rows):
    b, t, conv_dim = xbc.shape
    width = z.shape[2]
    d_width = gbch.shape[2] // 4
    n_heads = width // C_HEAD_DIM
    ssm_state, conv_c_state, conv_d_state = states
    assert t % rows == 0 and rows % 16 == 0 and (rows <= SSD_Q or rows % SSD_Q == 0) and n_heads <= LANES
    assert conv_dim % CONV_STRIP == 0 and d_width % CONV_STRIP == 0
    pad_lanes = lambda v: jnp.pad(v.astype(F32), (0, LANES - v.shape[0])).reshape(1, LANES)
    pad_hist = lambda s: jnp.pad(s.astype(F32), ((0, 0), (CONV_PAD - s.shape[1], 0), (0, 0)))
    h0t = jnp.swapaxes(ssm_state.astype(F32).reshape(b, width, C_STATE), 1, 2)
    expand = (jnp.arange(LANES)[:, None] == (jnp.arange(width)[None, :] // C_HEAD_DIM)).astype(BF16)
    expand3 = jnp.concatenate([expand] * 3, axis=0)
    full = lambda shape: pl.BlockSpec(shape, lambda bi, ci: (0,) * len(shape))
    per_b = lambda shape: pl.BlockSpec(shape, lambda bi, ci: (bi,) + (0,) * (len(shape) - 1))
    blk = lambda w, k=0: pl.BlockSpec((1, rows, w), lambda bi, ci: (bi, ci, k))
    y, yd, h_out, conv_out, dconv_out = pl.pallas_call(
        functools.partial(_ssd_kernel, rows=rows),
        grid=(b, t // rows),
        in_specs=[blk(conv_dim), blk(width), blk(LANES),
                  blk(d_width, 0), blk(d_width, 1), blk(d_width, 2), blk(d_width, 3),
                  per_b((1, CONV_PAD, conv_dim)), per_b((1, CONV_PAD, d_width)), per_b((1, C_STATE, width)),
                  full((C_CONV_W, conv_dim)), full((1, conv_dim)), full((D_CONV_W, d_width)),
                  full((1, LANES)), full((1, LANES)), full((1, width)), full((1, width)),
                  full((3 * LANES, width))],
        out_specs=[blk(width), blk(d_width), per_b((1, C_STATE, width)),
                   per_b((1, CONV_PAD, conv_dim)), per_b((1, CONV_PAD, d_width))],
        out_shape=[jax.ShapeDtypeStruct((b, t, width), BF16),
                   jax.ShapeDtypeStruct((b, t, d_width), BF16),
                   jax.ShapeDtypeStruct((b, C_STATE, width), F32),
                   jax.ShapeDtypeStruct((b, CONV_PAD, conv_dim), F32),
                   jax.ShapeDtypeStruct((b, CONV_PAD, d_width), F32)],
        scratch_shapes=[pltpu.VMEM((C_STATE, width), F32),
                        pltpu.VMEM((CONV_PAD + rows, conv_dim), F32),
                        pltpu.VMEM((CONV_PAD + rows, d_width), F32),
                        pltpu.VMEM((max(rows, SSD_Q), conv_dim), F32),
                        pltpu.VMEM((SSD_Q, width), F32)],
        compiler_params=_params("parallel", "arbitrary"),
        name="ssd",
    )(xbc, z, dtr, gbch, gbch, gbch, gbch, pad_hist(conv_c_state), pad_hist(conv_d_state), h0t,
      p["conv_c_w"].astype(F32), p["conv_c_b"].astype(F32).reshape(1, conv_dim), p["conv_d_w"].astype(F32),
      pad_lanes(p["dt_bias"]), pad_lanes(p["a_log"]),
      jnp.repeat(p["d_skip"].astype(F32), C_HEAD_DIM).reshape(1, width),
      p["c_norm"].astype(F32).reshape(1, width), expand3)
    new_state = jnp.swapaxes(h_out, 1, 2).reshape(b, n_heads, C_HEAD_DIM, C_STATE)
    return (y, yd, new_state, conv_out[:, CONV_PAD - (C_CONV_W - 1):, :],
            dconv_out[:, CONV_PAD - (D_CONV_W - 1):, :])


TN_WIDE = 1024
TN_NARROW = 512


def _even_weights(w_in, w_out):
    aw = w_out.shape[0] // 2
    bkv = (w_in.shape[1] - 6 * aw) // 2
    assert aw % TN_WIDE == 0 and (2 * bkv) % TN_NARROW == 0 and aw % TN_NARROW == 0
    return dict(w_in=w_in.astype(BF16), w_out=w_out.astype(BF16), aw=aw, bkv=bkv)


def _in_proj_even(hn, w, kv_dtype):
    aw, wb = w["aw"], w["w_in"]
    qkv_a = _matmul([hn], [(wb, 0)], kv_dtype, n=3 * aw, tn=TN_WIDE, name="in_proj_qkv_a")
    nq = aw // TN_NARROW
    qkv_b = _matmul([hn], [(wb, 0)], kv_dtype, n=aw + 2 * w["bkv"], tn=TN_NARROW, name="in_proj_qkv_b",
                    wcol=lambda j: jnp.where(j < nq, 4 * nq + j, 6 * nq + j - nq))
    ng = aw // TN_WIDE
    gates = _matmul([hn], [(wb, 0)], F32, n=2 * aw, tn=TN_WIDE, name="in_proj_gates",
                    wcol=lambda j: jnp.where(j < ng, 3 * ng + j, 5 * ng + j - ng))
    return qkv_a, qkv_b, gates


def _attn_layer(x2d, hn, w, bias_a, bias_b, sinks, caches, batch):
    aw, bkv = w["aw"], w["bkv"]
    heads = aw // HEAD_DIM
    qkv_a, qkv_b, gates = _in_proj_even(hn, w, BF16 if caches is None else F32)
    kvh_b = bkv // HEAD_DIM
    if caches is None:
        oa = _attn_prompt(qkv_a, gates, bias_a, None, heads=heads, group=1, n_back=A_BACK_CHUNKS,
                          q_col=0, k_col=heads, v_col=2 * heads, g_col=0)
        ob = _attn_prompt(qkv_b, gates, bias_b, sinks, heads=heads, group=heads // kvh_b, n_back=B_BACK_CHUNKS,
                          q_col=0, k_col=heads, v_col=heads + kvh_b, g_col=heads)
    else:
        ck_a, cv_a, ck_b, cv_b = caches
        flat = lambda c: c.astype(F32).reshape(c.shape[0], c.shape[1] * c.shape[2], c.shape[3])
        oa = _attn_step(qkv_a, flat(ck_a), flat(cv_a), gates, bias_a, None, batch=batch, heads=heads, group=1,
                        q_col=0, k_col=heads, v_col=2 * heads, g_blk=0)
        ob = _attn_step(qkv_b, flat(ck_b), flat(cv_b), gates, bias_b, sinks, batch=batch, heads=heads,
                        group=heads // kvh_b, q_col=0, k_col=heads, v_col=heads + kvh_b, g_blk=1)
    x_new = _matmul([oa, ob], [(w["w_out"], 0), (w["w_out"], 1)], F32, res=x2d, tn=TN_WIDE, name="out_proj_even")
    return x_new, qkv_a, qkv_b


def _odd_weights(w_in, w_out, c_width, conv_dim, c_heads):
    o1 = c_width + conv_dim
    assert c_width % TN_WIDE == 0 and conv_dim % TN_WIDE == 0
    wb = w_in.astype(BF16)
    return dict(
        w_in=wb,
        dt=jnp.pad(wb[:, o1:o1 + c_heads], ((0, 0), (0, LANES - c_heads))),
        gbch=wb[:, o1 + c_heads:],
        w_out=w_out.astype(BF16))


def _ssm_conv_layer(x2d, hn, w, p, states, batch, rows):
    c_width = p["c_norm"].shape[0]
    conv_dim = p["conv_c_w"].shape[1]
    d_width = p["conv_d_w"].shape[1]
    t = x2d.shape[0] // batch
    nz = c_width // TN_WIDE
    z = _matmul([hn], [(w["w_in"], 0)], F32, n=c_width, tn=TN_WIDE, name="in_proj_z").reshape(batch, t, c_width)
    xbc = _matmul([hn], [(w["w_in"], 0)], F32, n=conv_dim, tn=TN_WIDE, wcol=lambda j: nz + j,
                  name="in_proj_xbc").reshape(batch, t, conv_dim)
    dtr = _matmul([hn], [(w["dt"], 0)], F32, name="in_proj_dt").reshape(batch, t, LANES)
    gbch = _matmul([hn], [(w["gbch"], 0)], F32, tn=TN_WIDE, name="in_proj_gbch").reshape(batch, t, -1)
    yc, yd, new_ssm, new_cc, new_dc = _ssd(xbc, z, dtr, gbch, states, p, rows=rows)
    assert c_width % d_width == 0
    x_new = _matmul([yc.reshape(batch * t, c_width), yd.reshape(batch * t, d_width)],
                    [(w["w_out"], 0), (w["w_out"], c_width // d_width)], F32,
                    res=x2d, tn=TN_NARROW, name="out_proj_odd")
    return x_new, new_ssm, new_cc, new_dc


def kernel(x_prompt, x_sample, cache_a_k, cache_a_v, cache_b_k, cache_b_v, state_c_ssm, state_c_conv, state_d_conv, norm_even, w_in_even, w_out_even, rel_bias_a, sinks_b, norm_odd, w_in_odd, w_out_odd, conv_c_w, conv_c_b, dt_bias, a_log, d_skip, c_norm, conv_d_w, final_norm):
    bp, tp, d = x_prompt.shape
    bs, ts, _ = x_sample.shape
    assert bp == 1, "the prompt attention kernel handles one prompt sequence"
    dt_out = x_prompt.dtype
    depth = norm_even.shape[0] + norm_odd.shape[0]
    xp = x_prompt.astype(F32).reshape(bp * tp, d)
    xs = x_sample.astype(F32).reshape(bs * ts, d)
    keep_a = min(A_BACK_CHUNKS * CHUNK, tp)
    keep_b = min(B_BACK_CHUNKS * CHUNK, tp)
    outs = {k: [] for k in ("p_ak", "p_av", "p_bk", "p_bv", "p_ssm", "p_cc", "p_dc",
                            "s_ak", "s_av", "s_bk", "s_bv", "s_ssm", "s_cc", "s_dc")}
    for layer in range(depth):
        i = layer // 2
        if layer % 2 == 0:
            w = _even_weights(w_in_even[i], w_out_even[i])
            aw, bkv = w["aw"], w["bkv"]
            heads = aw // HEAD_DIM
            bias_a = _bias_tiles_a(rel_bias_a[i], ATTN_TILE_ROWS)
            bias_b = _bias_tiles_b(heads, ATTN_TILE_ROWS)
            hp = _rmsnorm(xp, norm_even[i], BF16)
            hs = _rmsnorm(xs, norm_even[i], BF16)
            xp_new, _, _ = _attn_layer(xp, hp, w, bias_a, bias_b, sinks_b[i], None, bp)
            kv_a = _matmul([hp[tp - keep_a:]], [(w["w_in"], 0)], F32, n=2 * aw, tn=TN_WIDE,
                           wcol=lambda j: aw // TN_WIDE + j, name="cache_kv_a")
            kv_b = _matmul([hp[tp - keep_b:]], [(w["w_in"], 0)], F32, n=2 * bkv, tn=TN_NARROW,
                           wcol=lambda j: 6 * aw // TN_NARROW + j, name="cache_kv_b")
            outs["p_ak"].append(kv_a[:, :aw].reshape(bp, keep_a, heads, HEAD_DIM))
            outs["p_av"].append(kv_a[:, aw:].reshape(bp, keep_a, heads, HEAD_DIM))
            outs["p_bk"].append(kv_b[:, :bkv].reshape(bp, keep_b, bkv // HEAD_DIM, HEAD_DIM))
            outs["p_bv"].append(kv_b[:, bkv:].reshape(bp, keep_b, bkv // HEAD_DIM, HEAD_DIM))
            xs_new, qkv_a, qkv_b = _attn_layer(xs, hs, w, bias_a, bias_b, sinks_b[i],
                                               (cache_a_k[i], cache_a_v[i], cache_b_k[i], cache_b_v[i]), bs)
            outs["s_ak"].append(qkv_a[:, aw:2 * aw].reshape(bs, ts, heads, HEAD_DIM))
            outs["s_av"].append(qkv_a[:, 2 * aw:].reshape(bs, ts, heads, HEAD_DIM))
            outs["s_bk"].append(qkv_b[:, aw:aw + bkv].reshape(bs, ts, bkv // HEAD_DIM, HEAD_DIM))
            outs["s_bv"].append(qkv_b[:, aw + bkv:].reshape(bs, ts, bkv // HEAD_DIM, HEAD_DIM))
            xp, xs = xp_new, xs_new
        else:
            c_width = c_norm.shape[1]
            conv_dim = conv_c_w.shape[2]
            c_heads = dt_bias.shape[1]
            d_width = conv_d_w.shape[2]
            w = _odd_weights(w_in_odd[i], w_out_odd[i], c_width, conv_dim, c_heads)
            p = dict(conv_c_w=conv_c_w[i], conv_c_b=conv_c_b[i], dt_bias=dt_bias[i], a_log=a_log[i],
                     d_skip=d_skip[i], c_norm=c_norm[i], conv_d_w=conv_d_w[i])
            zero = (jnp.zeros((bp, c_heads, C_HEAD_DIM, C_STATE), F32),
                    jnp.zeros((bp, C_CONV_W - 1, conv_dim), F32),
                    jnp.zeros((bp, D_CONV_W - 1, d_width), F32))
            hp = _rmsnorm(xp, norm_odd[i], BF16)
            hs = _rmsnorm(xs, norm_odd[i], BF16)
            xp, ssm, cc, dc = _ssm_conv_layer(xp, hp, w, p, zero, bp, min(SSD_ROWS_PER_STEP, tp))
            xs, sssm, scc, sdc = _ssm_conv_layer(xs, hs, w, p, (state_c_ssm[i], state_c_conv[i], state_d_conv[i]),
                                                 bs, ts)
            outs["p_ssm"].append(ssm.astype(dt_out)); outs["p_cc"].append(cc.astype(dt_out))
            outs["p_dc"].append(dc.astype(dt_out))
            outs["s_ssm"].append(sssm.astype(dt_out)); outs["s_cc"].append(scc.astype(dt_out))
            outs["s_dc"].append(sdc.astype(dt_out))
    y_prompt = _rmsnorm(xp, final_norm, dt_out).reshape(bp, tp, d)
    y_sample = _rmsnorm(xs, final_norm, dt_out).reshape(bs, ts, d)
    st = lambda k: jnp.stack(outs[k])
    return (y_prompt, y_sample, st("p_ak"), st("p_av"), st("p_bk"), st("p_bv"), st("p_ssm"), st("p_cc"), st("p_dc"),
            st("s_ak"), st("s_av"), st("s_bk"), st("s_bv"), st("s_ssm"), st("s_cc"), st("s_dc"))
```

```python
import functools

import jax
import jax.numpy as jnp
from jax import lax
from jax.experimental import pallas as pl
from jax.experimental.pallas import tpu as pltpu

F32 = jnp.float32
BF16 = jnp.bfloat16

CHUNK = 64
HEAD_DIM = 128
A_BACK_CHUNKS = 8
B_BACK_CHUNKS = 2
REL_CLIP = 128
C_HEAD_DIM = 64
C_GROUPS = 8
C_STATE = 128
C_CONV_W = 4
D_CONV_W = 3
RMS_EPS = 1e-6
NEG_INF = -1e30
ATTN_SCALE = HEAD_DIM ** -0.5
LOG2E = 1.4426950408889634

V7X_VMEM_BYTES = 64 * 1024 * 1024
VMEM_LIMIT_BYTES = V7X_VMEM_BYTES - 8 * 1024 * 1024
LANES = 128
SUBLANES = 8

ATTN_TILE_ROWS = 128
ATTN_ROWS_PER_STEP = 512
ATTN_HEADS_PER_STEP = 8
ATTN_HEADS_PER_UNIT = 1
SSD_Q = CHUNK
SSD_ROWS_PER_STEP = 2 * SSD_Q


def _params(*sem):
    return pltpu.CompilerParams(dimension_semantics=sem, vmem_limit_bytes=VMEM_LIMIT_BYTES)


def _silu(x):
    h = 0.5 * x
    return h + h * jnp.tanh(h)


def _split3(x):
    hi = x.astype(BF16)
    r1 = x - hi.astype(F32)
    mid = r1.astype(BF16)
    lo = (r1 - mid.astype(F32)).astype(BF16)
    return hi, mid, lo


def _rmsnorm_kernel(x_ref, g_ref, o_ref):
    x = x_ref[...]
    y = x * lax.rsqrt(jnp.mean(x * x, axis=-1, keepdims=True) + RMS_EPS)
    o_ref[...] = (y * g_ref[...]).astype(o_ref.dtype)


def _rmsnorm(x2d, gain, out_dtype, rows=512):
    m, d = x2d.shape
    rows = min(rows, m)
    assert m % rows == 0, (m, rows)
    return pl.pallas_call(
        _rmsnorm_kernel,
        grid=(m // rows,),
        in_specs=[pl.BlockSpec((rows, d), lambda i: (i, 0)), pl.BlockSpec((1, d), lambda i: (0, 0))],
        out_specs=pl.BlockSpec((rows, d), lambda i: (i, 0)),
        out_shape=jax.ShapeDtypeStruct((m, d), out_dtype),
        compiler_params=_params("parallel"),
        name="rmsnorm",
    )(x2d, gain.reshape(1, d).astype(F32))


def _matmul_kernel(*refs, n_lhs, has_res, silu_out):
    o_ref = refs[-1]
    acc = None
    for i in range(n_lhs):
        part = jnp.dot(refs[i][...], refs[n_lhs + i][...], preferred_element_type=F32)
        acc = part if acc is None else acc + part
    if has_res:
        acc = acc + refs[2 * n_lhs][...]
    if silu_out:
        acc = _silu(acc)
    o_ref[...] = acc.astype(o_ref.dtype)


def _matmul(lhs_list, w_list, out_dtype, *, n=None, wcol=None, res=None, silu_out=False, tm=1024, tn=1024,
            name="matmul"):
    m = lhs_list[0].shape[0]
    n = w_list[0][0].shape[1] if n is None else n
    wcol = (lambda j: j) if wcol is None else wcol
    tm, tn = min(tm, m), min(tn, n)
    assert m % tm == 0 and n % tn == 0, (m, n, tm, tn)
    in_specs = [pl.BlockSpec((tm, a.shape[1]), lambda i, j: (i, 0)) for a in lhs_list]
    in_specs += [pl.BlockSpec((a.shape[1], tn), lambda i, j, kblk=kblk: (kblk, wcol(j)))
                 for a, (_, kblk) in zip(lhs_list, w_list)]
    args = list(lhs_list) + [w for w, _ in w_list]
    if res is not None:
        in_specs.append(pl.BlockSpec((tm, tn), lambda i, j: (i, j)))
        args.append(res)
    return pl.pallas_call(
        functools.partial(_matmul_kernel, n_lhs=len(lhs_list), has_res=res is not None, silu_out=silu_out),
        grid=(m // tm, n // tn),
        in_specs=in_specs,
        out_specs=pl.BlockSpec((tm, tn), lambda i, j: (i, j)),
        out_shape=jax.ShapeDtypeStruct((m, n), out_dtype),
        compiler_params=_params("parallel", "arbitrary"),
        name=name,
    )(*args)


CONV_PAD = SUBLANES
CONV_STRIP = LANES


def _horner_dwconv(xs, w_ref, cols):
    acc = w_ref[0:1, cols] * xs
    for k in range(1, w_ref.shape[0]):
        acc = w_ref[k:k + 1, cols] * xs + pltpu.roll(acc, 1, 0)
    return acc


def _band_ok(i, j, var, bq, n_back):
    qc = i >> 6
    kc = j >> 6
    kpos = var * bq - n_back * CHUNK + j
    return (kc >= qc) & (kc <= qc + n_back) & (kpos >= 0)


def _bias_a_kernel(tab_ref, o_ref, *, bq, nk, n_back):
    var = pl.program_id(1)
    for c in range(nk // LANES):
        i = lax.broadcasted_iota(jnp.int32, (bq, LANES), 0)
        j = lax.broadcasted_iota(jnp.int32, (bq, LANES), 1) + c * LANES
        idx = jnp.clip(i - j + n_back * CHUNK, -REL_CLIP, REL_CLIP) + REL_CLIP
        lane = idx & (LANES - 1)
        seg = idx >> 7
        val = jnp.zeros((bq, LANES), F32)
        for s in range(tab_ref.shape[1]):
            tab = jnp.broadcast_to(tab_ref[0, s:s + 1, :], (bq, LANES))
            val = jnp.where(seg == s, jnp.take_along_axis(tab, lane, axis=1), val)
        o_ref[0, 0, :, c * LANES:(c + 1) * LANES] = jnp.where(_band_ok(i, j, var, bq, n_back), val * LOG2E, NEG_INF)


def _bias_b_kernel(slope_ref, o_ref, *, bq, nk, n_back):
    h = pl.program_id(0)
    var = pl.program_id(1)
    i = lax.broadcasted_iota(jnp.int32, (bq, nk), 0)
    j = lax.broadcasted_iota(jnp.int32, (bq, nk), 1)
    dist = jnp.abs(i - j + n_back * CHUNK).astype(F32)
    val = (-slope_ref[h]) * dist
    o_ref[0, 0] = jnp.where(_band_ok(i, j, var, bq, n_back), val * LOG2E, NEG_INF)


def _tile_geometry(n_back, bq):
    back_rows = n_back * CHUNK
    return back_rows + bq, -(-back_rows // bq) + 1


def _window_blocks(n_back, rows):
    back_rows = n_back * CHUNK
    kb = min(rows, back_rows)
    assert rows % kb == 0 and back_rows % kb == 0
    return kb, tuple(range(-(back_rows // kb), rows // kb))


def _bias_tiles_a(rel_table, bq):
    heads, width = rel_table.shape
    nseg = -(-width // LANES)
    tab = jnp.pad(rel_table.astype(F32), ((0, 0), (0, nseg * LANES - width))).reshape(heads, nseg, LANES)
    nk, nvar = _tile_geometry(A_BACK_CHUNKS, bq)
    return pl.pallas_call(
        functools.partial(_bias_a_kernel, bq=bq, nk=nk, n_back=A_BACK_CHUNKS),
        grid=(heads, nvar),
        in_specs=[pl.BlockSpec((1, nseg, LANES), lambda h, v: (h, 0, 0))],
        out_specs=pl.BlockSpec((1, 1, bq, nk), lambda h, v: (h, v, 0, 0)),
        out_shape=jax.ShapeDtypeStruct((heads, nvar, bq, nk), F32),
        compiler_params=_params("parallel", "parallel"),
        name="bias_a",
    )(tab)


def _bias_tiles_b(heads, bq):
    slopes = 2.0 ** (-8.0 * jnp.arange(1, heads + 1, dtype=F32) / heads)
    nk, nvar = _tile_geometry(B_BACK_CHUNKS, bq)
    return pl.pallas_call(
        functools.partial(_bias_b_kernel, bq=bq, nk=nk, n_back=B_BACK_CHUNKS),
        grid=(heads, nvar),
        in_specs=[pl.BlockSpec(memory_space=pltpu.SMEM)],
        out_specs=pl.BlockSpec((1, 1, bq, nk), lambda h, v: (h, v, 0, 0)),
        out_shape=jax.ShapeDtypeStruct((heads, nvar, bq, nk), F32),
        compiler_params=_params("parallel", "parallel"),
        name="bias_b",
    )(slopes)


def _softmax_pv(q, keys, vals, biases, sink, gate):
    logits = [jnp.einsum("nqd,nkd->nqk", q, k, preferred_element_type=F32) * (ATTN_SCALE * LOG2E) + b
              for k, b in zip(keys, biases)]
    m = functools.reduce(jnp.maximum, [jnp.max(s, axis=-1, keepdims=True) for s in logits])
    if sink is not None:
        m = jnp.maximum(m, sink)
    probs = [jnp.exp2(s - m) for s in logits]
    denom = functools.reduce(jnp.add, [jnp.sum(p, axis=-1, keepdims=True) for p in probs])
    if sink is not None:
        denom = denom + jnp.exp2(sink - m)
    o = functools.reduce(jnp.add, [jnp.einsum("nqk,nkd->nqd", p.astype(BF16), v, preferred_element_type=F32)
                                   for p, v in zip(probs, vals)])
    return (o / denom) * gate


def _head_cols(i):
    return slice(i * HEAD_DIM, (i + 1) * HEAD_DIM)


def _attn_prompt_kernel(*refs, nblk, bq, nk, ntile, hps, kvps, hpu, has_sink):
    sink_ref, q_ref = refs[0], refs[1]
    k_refs = refs[2:2 + nblk]
    v_refs = refs[2 + nblk:2 + 2 * nblk]
    g_ref = refs[2 + 2 * nblk]
    b_refs = refs[3 + 2 * nblk:3 + 2 * nblk + ntile]
    o_ref = refs[-1]
    group = hps // kvps
    head0 = pl.program_id(0) * hps
    units = [(kv, r, kv * group + c, hpu) for kv in range(kvps) for r in range(ntile) for c in range(0, group, hpu)]

    def window(blk_refs, kv):
        return jnp.concatenate([ref[:, _head_cols(kv)] for ref in blk_refs], axis=0).astype(BF16)

    def stacked(ref, r, h0, nh):
        return jnp.concatenate([ref[r * bq:(r + 1) * bq, _head_cols(h0 + g)] for g in range(nh)], axis=0)

    kwin = [window(k_refs, kv) for kv in range(kvps)]
    vwin = [window(v_refs, kv) for kv in range(kvps)]
    q = jnp.stack([stacked(q_ref, r, h0, nh) for _, r, h0, nh in units]).astype(BF16)
    k = jnp.stack([kwin[kv][r * bq:r * bq + nk] for kv, r, _, _ in units])
    v = jnp.stack([vwin[kv][r * bq:r * bq + nk] for kv, r, _, _ in units])
    bias = jnp.stack([b_refs[r][h0:h0 + nh, 0].reshape(nh * bq, nk) for _, r, h0, nh in units])
    gate = jnp.stack([stacked(g_ref, r, h0, nh) for _, r, h0, nh in units])
    sink = None
    if has_sink:
        sink = jnp.stack([jnp.concatenate([jnp.full((bq, 1), sink_ref[head0 + h0 + g], F32)
                                           for g in range(nh)], axis=0) for _, _, h0, nh in units])
    out = _softmax_pv(q, [k], [v], [bias], sink, gate).astype(o_ref.dtype)
    for n, (_, r, h0, nh) in enumerate(units):
        for g in range(nh):
            o_ref[r * bq:(r + 1) * bq, _head_cols(h0 + g)] = out[n, g * bq:(g + 1) * bq, :]


def _attn_prompt(qkv, gates, bias, sinks, *, heads, group, n_back, q_col, k_col, v_col, g_col):
    t = qkv.shape[0]
    rows, bq = ATTN_ROWS_PER_STEP, bias.shape[2]
    ntile = rows // bq
    hps = ATTN_HEADS_PER_STEP
    assert hps % group == 0 and heads % hps == 0
    kvps = hps // group
    nk, nvar = _tile_geometry(n_back, bq)
    kb, offs = _window_blocks(n_back, rows)
    assert t % rows == 0 and bias.shape == (heads, nvar, bq, nk)
    assert q_col % hps == 0 and g_col % hps == 0 and k_col % kvps == 0 and v_col % kvps == 0
    has_sink = sinks is not None
    sink_arr = sinks.astype(F32) * LOG2E if has_sink else jnp.zeros((heads,), F32)

    def kv_spec(col, off):
        return pl.BlockSpec((kb, kvps * HEAD_DIM),
                            lambda hb, qi: (jnp.maximum(qi * (rows // kb) + off, 0), col // kvps + hb))

    def bias_spec(r):
        return pl.BlockSpec((hps, 1, bq, nk), lambda hb, qi: (hb, jnp.minimum(qi * ntile + r, nvar - 1), 0, 0))

    in_specs = [pl.BlockSpec(memory_space=pltpu.SMEM),
                pl.BlockSpec((rows, hps * HEAD_DIM), lambda hb, qi: (qi, q_col // hps + hb))]
    in_specs += [kv_spec(k_col, off) for off in offs]
    in_specs += [kv_spec(v_col, off) for off in offs]
    in_specs += [pl.BlockSpec((rows, hps * HEAD_DIM), lambda hb, qi: (qi, g_col // hps + hb))]
    in_specs += [bias_spec(r) for r in range(ntile)]
    return pl.pallas_call(
        functools.partial(_attn_prompt_kernel, nblk=len(offs), bq=bq, nk=nk, ntile=ntile, hps=hps, kvps=kvps,
                          hpu=min(group, ATTN_HEADS_PER_UNIT), has_sink=has_sink),
        grid=(heads // hps, t // rows),
        in_specs=in_specs,
        out_specs=pl.BlockSpec((rows, hps * HEAD_DIM), lambda hb, qi: (qi, hb)),
        out_shape=jax.ShapeDtypeStruct((t, heads * HEAD_DIM), BF16),
        compiler_params=_params("parallel", "arbitrary"),
        name="attn_prompt",
    )(sink_arr, qkv, *([qkv] * (2 * len(offs))), gates, *([bias] * ntile))


def _attn_step_kernel(sink_ref, qkv_ref, ck_ref, cv_ref, g_ref, b_ref, o_ref, *,
                      heads, group, cache_len, q_col, k_col, v_col, has_sink):
    t = qkv_ref.shape[0]
    kv_heads = heads // group

    def head_block(ref2d, col):
        return ref2d[:, _head_cols(col)]

    def per_kv(fn):
        return jnp.stack([fn(kv) for kv in range(kv_heads)])

    def grouped(ref2d, col0):
        return per_kv(lambda kv: jnp.concatenate(
            [head_block(ref2d, col0 + kv * group + g) for g in range(group)], axis=0))

    def cached(ref, kv):
        return ref[0, pl.ds(kv, cache_len, stride=kv_heads), :].astype(BF16)

    q = grouped(qkv_ref, q_col).astype(BF16)
    keys = (per_kv(lambda kv: cached(ck_ref, kv)),
            per_kv(lambda kv: head_block(qkv_ref, k_col + kv).astype(BF16)))
    vals = (per_kv(lambda kv: cached(cv_ref, kv)),
            per_kv(lambda kv: head_block(qkv_ref, v_col + kv).astype(BF16)))
    bias = b_ref[:, 0].reshape(kv_heads, group * t, b_ref.shape[3])
    biases = (bias[:, :, 0:cache_len], bias[:, :, cache_len:cache_len + t])
    sink = None
    if has_sink:
        sink = per_kv(lambda kv: jnp.concatenate(
            [jnp.full((t, 1), sink_ref[kv * group + g], F32) for g in range(group)], axis=0))
    out = _softmax_pv(q, keys, vals, biases, sink, grouped(g_ref, 0)).astype(o_ref.dtype)
    for kv in range(kv_heads):
        for g in range(group):
            h = kv * group + g
            o_ref[:, h * HEAD_DIM:(h + 1) * HEAD_DIM] = out[kv, g * t:(g + 1) * t, :]


def _attn_step(qkv, cache_k, cache_v, gates, bias, sinks, *, batch, heads, group, q_col, k_col, v_col, g_blk):
    rows = qkv.shape[0]
    t = rows // batch
    cache_rows = cache_k.shape[1]
    cache_len = cache_rows // (heads // group)
    nvar, bq, nk = bias.shape[1:]
    assert t <= CHUNK and cache_len + t <= nk and t <= bq
    has_sink = sinks is not None
    sink_arr = sinks.astype(F32) * LOG2E if has_sink else jnp.zeros((heads,), F32)
    width = heads * HEAD_DIM
    return pl.pallas_call(
        functools.partial(_attn_step_kernel, heads=heads, group=group, cache_len=cache_len,
                          q_col=q_col, k_col=k_col, v_col=v_col, has_sink=has_sink),
        grid=(batch,),
        in_specs=[pl.BlockSpec(memory_space=pltpu.SMEM),
                  pl.BlockSpec((t, qkv.shape[1]), lambda b: (b, 0)),
                  pl.BlockSpec((1, cache_rows, HEAD_DIM), lambda b: (b, 0, 0)),
                  pl.BlockSpec((1, cache_rows, HEAD_DIM), lambda b: (b, 0, 0)),
                  pl.BlockSpec((t, width), lambda b: (b, g_blk)),
                  pl.BlockSpec((heads, 1, t, nk), lambda b: (0, nvar - 1, 0, 0))],
        out_specs=pl.BlockSpec((t, width), lambda b: (b, 0)),
        out_shape=jax.ShapeDtypeStruct((rows, width), BF16),
        compiler_params=_params("parallel"),
        name="attn_step",
    )(sink_arr, qkv, cache_k, cache_v, gates, bias)


def _ssd_kernel(xbc_ref, zs_ref, dtr_ref, gs_ref, b_ref, c2_ref, h2_ref, conv0_ref, dconv0_ref, h0_ref, cw_ref, cb_ref,
                dw_ref, dtb_ref, alog_ref, dsk_ref, cng_ref, exp_ref,
                y_ref, yd_ref, hout_ref, convout_ref, dconvout_ref, h_s, xpad_s, upad_s, act_s, gated_s, *, rows):
    c = pl.program_id(1)
    q = SSD_Q
    width = y_ref.shape[2]
    gw = width // C_GROUPS
    pad0 = CONV_PAD

    @pl.when(c == 0)
    def _():
        h_s[...] = h0_ref[0]
        upad_s[0:pad0, :] = dconv0_ref[0]
        xpad_s[0:pad0, :] = conv0_ref[0]

    upad_s[pad0:pad0 + rows, :] = c2_ref[0] * h2_ref[0]
    for s0 in range(0, upad_s.shape[1], CONV_STRIP):
        cols = slice(s0, s0 + CONV_STRIP)
        v = _horner_dwconv(upad_s[:, cols], dw_ref, cols)[pad0:pad0 + rows]
        yd_ref[0, :, cols] = (b_ref[0, :, cols] * v * gs_ref[0, :, cols]).astype(yd_ref.dtype)
    dconvout_ref[0] = upad_s[rows:rows + pad0, :]
    upad_s[0:pad0, :] = upad_s[rows:rows + pad0, :]

    xpad_s[pad0:pad0 + rows, :] = xbc_ref[0]
    for s0 in range(0, xpad_s.shape[1], CONV_STRIP):
        cols = slice(s0, s0 + CONV_STRIP)
        conv = _horner_dwconv(xpad_s[:, cols], cw_ref, cols)[pad0:pad0 + rows]
        act_s[0:rows, cols] = _silu(conv + cb_ref[:, cols])
    if rows < q:
        act_s[rows:q, :] = jnp.zeros((q - rows, act_s.shape[1]), F32)
    convout_ref[0] = xpad_s[rows:rows + pad0, :]
    xpad_s[0:pad0, :] = xpad_s[rows:rows + pad0, :]

    def act(r0, cols):
        return act_s[r0:r0 + q, cols]

    row = lax.broadcasted_iota(jnp.int32, (q, LANES), 0)
    lane = lax.broadcasted_iota(jnp.int32, (q, LANES), 1)
    causal = row >= (lane & (C_HEAD_DIM - 1))
    low_half = lane < C_HEAD_DIM
    diag_g = (lax.broadcasted_iota(jnp.int32, (q, gw), 0)
              == (lax.broadcasted_iota(jnp.int32, (q, gw), 1) & (C_HEAD_DIM - 1)))
    tril = (lax.broadcasted_iota(jnp.int32, (q, q), 0) >= lax.broadcasted_iota(jnp.int32, (q, q), 1)).astype(BF16)
    tril3 = jnp.concatenate([tril] * 3, axis=1)
    neg_a = -jnp.exp(alog_ref[...]) * LOG2E
    valid = min(rows, q)

    def pad_rows(v):
        return v if valid == q else jnp.concatenate([v, jnp.zeros((q - valid, v.shape[1]), F32)], axis=0)

    for r0 in range(0, rows, q):
        dt = jnp.where(row < valid, jax.nn.softplus(pad_rows(dtr_ref[0, r0:r0 + valid, :]) + dtb_ref[...]), 0.0)
        cum = jnp.dot(tril3, jnp.concatenate(_split3(dt * neg_a), axis=0), preferred_element_type=F32)
        spread = jnp.concatenate(_split3(jnp.concatenate([cum, dt], axis=0)), axis=1)
        ssq = jnp.zeros((q, 1), F32)
        for g in range(C_GROUPS):
            cols = slice(g * gw, (g + 1) * gw)
            ex = jnp.dot(spread, exp_ref[:, cols], preferred_element_type=F32)
            cexp, dtx = ex[0:q], ex[q:2 * q]
            x_g = act(r0, cols)
            b_g = act(r0, slice(width + g * C_STATE, width + (g + 1) * C_STATE)).astype(BF16)
            c_g = act(r0, slice(width + (C_GROUPS + g) * C_STATE, width + (C_GROUPS + g + 1) * C_STATE)).astype(BF16)
            xdt = x_g * dtx
            cum_row = jnp.sum(jnp.where(diag_g, cexp, 0.0), axis=0, keepdims=True)
            cum_last = cexp[q - 1:q, :]
            scores = lax.dot_general(c_g, jnp.concatenate([b_g, b_g], axis=0), (((1,), (1,)), ((), ())),
                                     preferred_element_type=F32)
            h_g = h_s[:, cols]
            y_off = jnp.dot(c_g, h_g.astype(BF16), preferred_element_type=F32) * jnp.exp2(cexp)
            y_diag = []
            for j in range(gw // LANES):
                pc = slice(j * LANES, (j + 1) * LANES)
                seg = cexp[:, pc] - cum_row[:, pc]
                lmat = jnp.exp2(jnp.where(causal, seg, NEG_INF))
                xp = xdt[:, pc]
                blockdiag = jnp.concatenate([jnp.where(low_half, xp, 0.0), jnp.where(low_half, 0.0, xp)], axis=0)
                y_diag.append(jnp.dot((scores * lmat).astype(BF16), blockdiag.astype(BF16),
                                      preferred_element_type=F32))
            y = jnp.concatenate(y_diag, axis=1) + y_off + dsk_ref[:, cols] * x_g
            xw = (xdt * jnp.exp2(cum_last - cexp)).astype(BF16)
            h_s[:, cols] = h_g * jnp.exp2(cum_last) + lax.dot_general(
                b_g, xw, (((0,), (0,)), ((), ())), preferred_element_type=F32)
            gated = y * pad_rows(zs_ref[0, r0:r0 + valid, cols])
            gated_s[:, cols] = gated
            ssq = ssq + jnp.sum(gated * gated, axis=-1, keepdims=True)
        inv = lax.rsqrt(ssq * (1.0 / width) + RMS_EPS)
        y_ref[0, r0:r0 + valid, :] = ((gated_s[...] * inv)[0:valid] * cng_ref[...]).astype(y_ref.dtype)

    @pl.when(c == pl.num_programs(1) - 1)
    def _():
        hout_ref[0] = h_s[...]


def _ssd(xbc, zs, dtr, gs, bch, states, p, *, rows):
    b, t, conv_dim = xbc.shape
    width = zs.shape[2]
    d_width = gs.shape[2]
    n_heads = width // C_HEAD_DIM
    ssm_state, conv_c_state, conv_d_state = states
    assert t % rows == 0 and rows % 16 == 0 and (rows <= SSD_Q or rows % SSD_Q == 0) and n_heads <= LANES
    assert conv_dim % CONV_STRIP == 0 and d_width % CONV_STRIP == 0
    pad_lanes = lambda v: jnp.pad(v.astype(F32), (0, LANES - v.shape[0])).reshape(1, LANES)
    pad_hist = lambda s: jnp.pad(s.astype(F32), ((0, 0), (CONV_PAD - s.shape[1], 0), (0, 0)))
    h0t = jnp.swapaxes(ssm_state.astype(F32).reshape(b, width, C_STATE), 1, 2)
    expand = (jnp.arange(LANES)[:, None] == (jnp.arange(width)[None, :] // C_HEAD_DIM)).astype(BF16)
    expand3 = jnp.concatenate([expand] * 3, axis=0)
    full = lambda shape: pl.BlockSpec(shape, lambda bi, ci: (0,) * len(shape))
    per_b = lambda shape: pl.BlockSpec(shape, lambda bi, ci: (bi,) + (0,) * (len(shape) - 1))
    blk = lambda w, k=0: pl.BlockSpec((1, rows, w), lambda bi, ci: (bi, ci, k))
    y, yd, h_out, conv_out, dconv_out = pl.pallas_call(
        functools.partial(_ssd_kernel, rows=rows),
        grid=(b, t // rows),
        in_specs=[blk(conv_dim), blk(width), blk(LANES),
                  blk(d_width), blk(d_width, 0), blk(d_width, 1), blk(d_width, 2),
                  per_b((1, CONV_PAD, conv_dim)), per_b((1, CONV_PAD, d_width)), per_b((1, C_STATE, width)),
                  full((C_CONV_W, conv_dim)), full((1, conv_dim)), full((D_CONV_W, d_width)),
                  full((1, LANES)), full((1, LANES)), full((1, width)), full((1, width)),
                  full((3 * LANES, width))],
        out_specs=[blk(width), blk(d_width), per_b((1, C_STATE, width)),
                   per_b((1, CONV_PAD, conv_dim)), per_b((1, CONV_PAD, d_width))],
        out_shape=[jax.ShapeDtypeStruct((b, t, width), BF16),
                   jax.ShapeDtypeStruct((b, t, d_width), BF16),
                   jax.ShapeDtypeStruct((b, C_STATE, width), F32),
                   jax.ShapeDtypeStruct((b, CONV_PAD, conv_dim), F32),
                   jax.ShapeDtypeStruct((b, CONV_PAD, d_width), F32)],
        scratch_shapes=[pltpu.VMEM((C_STATE, width), F32),
                        pltpu.VMEM((CONV_PAD + rows, conv_dim), F32),
                        pltpu.VMEM((CONV_PAD + rows, d_width), F32),
                        pltpu.VMEM((max(rows, SSD_Q), conv_dim), F32),
                        pltpu.VMEM((SSD_Q, width), F32)],
        compiler_params=_params("parallel", "arbitrary"),
        name="ssd",
    )(xbc, zs, dtr, gs, bch, bch, bch, pad_hist(conv_c_state), pad_hist(conv_d_state), h0t,
      p["conv_c_w"].astype(F32), p["conv_c_b"].astype(F32).reshape(1, conv_dim), p["conv_d_w"].astype(F32),
      pad_lanes(p["dt_bias"]), pad_lanes(p["a_log"]),
      jnp.repeat(p["d_skip"].astype(F32), C_HEAD_DIM).reshape(1, width),
      p["c_norm"].astype(F32).reshape(1, width), expand3)
    new_state = jnp.swapaxes(h_out, 1, 2).reshape(b, n_heads, C_HEAD_DIM, C_STATE)
    return (y, yd, new_state, conv_out[:, CONV_PAD - (C_CONV_W - 1):, :],
            dconv_out[:, CONV_PAD - (D_CONV_W - 1):, :])


TN_WIDE = 1024
TN_NARROW = 512


def _even_weights(w_in, w_out):
    aw = w_out.shape[0] // 2
    bkv = (w_in.shape[1] - 6 * aw) // 2
    assert aw % TN_WIDE == 0 and (2 * bkv) % TN_NARROW == 0 and aw % TN_NARROW == 0
    return dict(w_in=w_in.astype(BF16), w_out=w_out.astype(BF16), aw=aw, bkv=bkv)


def _in_proj_even(hn, w, kv_dtype):
    aw, wb = w["aw"], w["w_in"]
    qkv_a = _matmul([hn], [(wb, 0)], kv_dtype, n=3 * aw, tn=TN_WIDE, name="in_proj_qkv_a")
    nq = aw // TN_NARROW
    qkv_b = _matmul([hn], [(wb, 0)], kv_dtype, n=aw + 2 * w["bkv"], tn=TN_NARROW, name="in_proj_qkv_b",
                    wcol=lambda j: jnp.where(j < nq, 4 * nq + j, 6 * nq + j - nq))
    ng = aw // TN_WIDE
    gates = _matmul([hn], [(wb, 0)], F32, n=2 * aw, tn=TN_WIDE, name="in_proj_gates", silu_out=True,
                    wcol=lambda j: jnp.where(j < ng, 3 * ng + j, 5 * ng + j - ng))
    return qkv_a, qkv_b, gates


def _attn_layer(x2d, hn, w, bias_a, bias_b, sinks, caches, batch):
    aw, bkv = w["aw"], w["bkv"]
    heads = aw // HEAD_DIM
    qkv_a, qkv_b, gates = _in_proj_even(hn, w, BF16 if caches is None else F32)
    kvh_b = bkv // HEAD_DIM
    if caches is None:
        oa = _attn_prompt(qkv_a, gates, bias_a, None, heads=heads, group=1, n_back=A_BACK_CHUNKS,
                          q_col=0, k_col=heads, v_col=2 * heads, g_col=0)
        ob = _attn_prompt(qkv_b, gates, bias_b, sinks, heads=heads, group=heads // kvh_b, n_back=B_BACK_CHUNKS,
                          q_col=0, k_col=heads, v_col=heads + kvh_b, g_col=heads)
    else:
        ck_a, cv_a, ck_b, cv_b = caches
        flat = lambda c: c.astype(F32).reshape(c.shape[0], c.shape[1] * c.shape[2], c.shape[3])
        oa = _attn_step(qkv_a, flat(ck_a), flat(cv_a), gates, bias_a, None, batch=batch, heads=heads, group=1,
                        q_col=0, k_col=heads, v_col=2 * heads, g_blk=0)
        ob = _attn_step(qkv_b, flat(ck_b), flat(cv_b), gates, bias_b, sinks, batch=batch, heads=heads,
                        group=heads // kvh_b, q_col=0, k_col=heads, v_col=heads + kvh_b, g_blk=1)
    x_new = _matmul([oa, ob], [(w["w_out"], 0), (w["w_out"], 1)], F32, res=x2d, tn=TN_WIDE, name="out_proj_even")
    return x_new, qkv_a, qkv_b


def _odd_weights(w_in, w_out, c_width, conv_dim, c_heads):
    o1 = c_width + conv_dim
    assert c_width % TN_WIDE == 0 and conv_dim % TN_WIDE == 0
    wb = w_in.astype(BF16)
    return dict(
        w_in=wb,
        dt=jnp.pad(wb[:, o1:o1 + c_heads], ((0, 0), (0, LANES - c_heads))),
        gbch=wb[:, o1 + c_heads:],
        w_out=w_out.astype(BF16))


def _ssm_conv_layer(x2d, hn, w, p, states, batch, rows):
    c_width = p["c_norm"].shape[0]
    conv_dim = p["conv_c_w"].shape[1]
    d_width = p["conv_d_w"].shape[1]
    t = x2d.shape[0] // batch
    nz = c_width // TN_WIDE
    assert c_width % d_width == 0 and d_width % TN_WIDE == 0
    zs = _matmul([hn], [(w["w_in"], 0)], F32, n=c_width, tn=TN_WIDE, silu_out=True,
                 name="in_proj_z").reshape(batch, t, c_width)
    xbc = _matmul([hn], [(w["w_in"], 0)], F32, n=conv_dim, tn=TN_WIDE, wcol=lambda j: nz + j,
                  name="in_proj_xbc").reshape(batch, t, conv_dim)
    dtr = _matmul([hn], [(w["dt"], 0)], F32, name="in_proj_dt").reshape(batch, t, LANES)
    ng = d_width // TN_WIDE
    gs = _matmul([hn], [(w["gbch"], 0)], F32, n=d_width, tn=TN_WIDE, silu_out=True,
                 name="in_proj_gate_d").reshape(batch, t, d_width)
    bch = _matmul([hn], [(w["gbch"], 0)], F32, n=3 * d_width, tn=TN_WIDE, wcol=lambda j: ng + j,
                  name="in_proj_bch").reshape(batch, t, 3 * d_width)
    yc, yd, new_ssm, new_cc, new_dc = _ssd(xbc, zs, dtr, gs, bch, states, p, rows=rows)
    x_new = _matmul([yc.reshape(batch * t, c_width), yd.reshape(batch * t, d_width)],
                    [(w["w_out"], 0), (w["w_out"], c_width // d_width)], F32,
                    res=x2d, tn=TN_NARROW, name="out_proj_odd")
    return x_new, new_ssm, new_cc, new_dc


def kernel(x_prompt, x_sample, cache_a_k, cache_a_v, cache_b_k, cache_b_v, state_c_ssm, state_c_conv, state_d_conv, norm_even, w_in_even, w_out_even, rel_bias_a, sinks_b, norm_odd, w_in_odd, w_out_odd, conv_c_w, conv_c_b, dt_bias, a_log, d_skip, c_norm, conv_d_w, final_norm):
    bp, tp, d = x_prompt.shape
    bs, ts, _ = x_sample.shape
    assert bp == 1, "the prompt attention kernel handles one prompt sequence"
    dt_out = x_prompt.dtype
    depth = norm_even.shape[0] + norm_odd.shape[0]
    xp = x_prompt.astype(F32).reshape(bp * tp, d)
    xs = x_sample.astype(F32).reshape(bs * ts, d)
    keep_a = min(A_BACK_CHUNKS * CHUNK, tp)
    keep_b = min(B_BACK_CHUNKS * CHUNK, tp)
    outs = {k: [] for k in ("p_ak", "p_av", "p_bk", "p_bv", "p_ssm", "p_cc", "p_dc",
                            "s_ak", "s_av", "s_bk", "s_bv", "s_ssm", "s_cc", "s_dc")}
    for layer in range(depth):
        i = layer // 2
        if layer % 2 == 0:
            w = _even_weights(w_in_even[i], w_out_even[i])
            aw, bkv = w["aw"], w["bkv"]
            heads = aw // HEAD_DIM
            bias_a = _bias_tiles_a(rel_bias_a[i], ATTN_TILE_ROWS)
            bias_b = _bias_tiles_b(heads, ATTN_TILE_ROWS)
            hp = _rmsnorm(xp, norm_even[i], BF16)
            hs = _rmsnorm(xs, norm_even[i], BF16)
            xp_new, _, _ = _attn_layer(xp, hp, w, bias_a, bias_b, sinks_b[i], None, bp)
            kv_a = _matmul([hp[tp - keep_a:]], [(w["w_in"], 0)], F32, n=2 * aw, tn=TN_WIDE,
                           wcol=lambda j: aw // TN_WIDE + j, name="cache_kv_a")
            kv_b = _matmul([hp[tp - keep_b:]], [(w["w_in"], 0)], F32, n=2 * bkv, tn=TN_NARROW,
                           wcol=lambda j: 6 * aw // TN_NARROW + j, name="cache_kv_b")
            outs["p_ak"].append(kv_a[:, :aw].reshape(bp, keep_a, heads, HEAD_DIM))
            outs["p_av"].append(kv_a[:, aw:].reshape(bp, keep_a, heads, HEAD_DIM))
            outs["p_bk"].append(kv_b[:, :bkv].reshape(bp, keep_b, bkv // HEAD_DIM, HEAD_DIM))
            outs["p_bv"].append(kv_b[:, bkv:].reshape(bp, keep_b, bkv // HEAD_DIM, HEAD_DIM))
            xs_new, qkv_a, qkv_b = _attn_layer(xs, hs, w, bias_a, bias_b, sinks_b[i],
                                               (cache_a_k[i], cache_a_v[i], cache_b_k[i], cache_b_v[i]), bs)
            outs["s_ak"].append(qkv_a[:, aw:2 * aw].reshape(bs, ts, heads, HEAD_DIM))
            outs["s_av"].append(qkv_a[:, 2 * aw:].reshape(bs, ts, heads, HEAD_DIM))
            outs["s_bk"].append(qkv_b[:, aw:aw + bkv].reshape(bs, ts, bkv // HEAD_DIM, HEAD_DIM))
            outs["s_bv"].append(qkv_b[:, aw + bkv:].reshape(bs, ts, bkv // HEAD_DIM, HEAD_DIM))
            xp, xs = xp_new, xs_new
        else:
            c_width = c_norm.shape[1]
            conv_dim = conv_c_w.shape[2]
            c_heads = dt_bias.shape[1]
            d_width = conv_d_w.shape[2]
            w = _odd_weights(w_in_odd[i], w_out_odd[i], c_width, conv_dim, c_heads)
            p = dict(conv_c_w=conv_c_w[i], conv_c_b=conv_c_b[i], dt_bias=dt_bias[i], a_log=a_log[i],
                     d_skip=d_skip[i], c_norm=c_norm[i], conv_d_w=conv_d_w[i])
            zero = (jnp.zeros((bp, c_heads, C_HEAD_DIM, C_STATE), F32),
                    jnp.zeros((bp, C_CONV_W - 1, conv_dim), F32),
                    jnp.zeros((bp, D_CONV_W - 1, d_width), F32))
            hp = _rmsnorm(xp, norm_odd[i], BF16)
            hs = _rmsnorm(xs, norm_odd[i], BF16)
            xp, ssm, cc, dc = _ssm_conv_layer(xp, hp, w, p, zero, bp, min(SSD_ROWS_PER_STEP, tp))
            xs, sssm, scc, sdc = _ssm_conv_layer(xs, hs, w, p, (state_c_ssm[i], state_c_conv[i], state_d_conv[i]),
                                                 bs, ts)
            outs["p_ssm"].append(ssm.astype(dt_out)); outs["p_cc"].append(cc.astype(dt_out))
            outs["p_dc"].append(dc.astype(dt_out))
            outs["s_ssm"].append(sssm.astype(dt_out)); outs["s_cc"].append(scc.astype(dt_out))
            outs["s_dc"].append(sdc.astype(dt_out))
    y_prompt = _rmsnorm(xp, final_norm, dt_out).reshape(bp, tp, d)
    y_sample = _rmsnorm(xs, final_norm, dt_out).reshape(bs, ts, d)
    st = lambda k: jnp.stack(outs[k])
    return (y_prompt, y_sample, st("p_ak"), st("p_av"), st("p_bk"), st("p_bv"), st("p_ssm"), st("p_cc"), st("p_dc"),
            st("s_ak"), st("s_av"), st("s_bk"), st("s_bv"), st("s_ssm"), st("s_cc"), st("s_dc"))
```

```python
import functools

import jax
import jax.numpy as jnp
from jax import lax
from jax.experimental import pallas as pl
from jax.experimental.pallas import tpu as pltpu

F32 = jnp.float32
BF16 = jnp.bfloat16

CHUNK = 64
HEAD_DIM = 128
A_BACK_CHUNKS = 8
B_BACK_CHUNKS = 2
REL_CLIP = 128
C_HEAD_DIM = 64
C_GROUPS = 8
C_STATE = 128
C_CONV_W = 4
D_CONV_W = 3
RMS_EPS = 1e-6
NEG_INF = -1e30
ATTN_SCALE = HEAD_DIM ** -0.5
LOG2E = 1.4426950408889634

V7X_VMEM_BYTES = 64 * 1024 * 1024
VMEM_LIMIT_BYTES = V7X_VMEM_BYTES - 8 * 1024 * 1024
LANES = 128
SUBLANES = 8

ATTN_TILE_ROWS = 128
ATTN_ROWS_PER_STEP = 512
ATTN_HEADS_PER_STEP = 8
ATTN_HEADS_PER_UNIT = 1
SSD_Q = CHUNK
SSD_ROWS_PER_STEP = 2 * SSD_Q


def _params(*sem):
    return pltpu.CompilerParams(dimension_semantics=sem, vmem_limit_bytes=VMEM_LIMIT_BYTES)


def _silu(x):
    h = 0.5 * x
    return h + h * jnp.tanh(h)


def _split3(x):
    hi = x.astype(BF16)
    r1 = x - hi.astype(F32)
    mid = r1.astype(BF16)
    lo = (r1 - mid.astype(F32)).astype(BF16)
    return hi, mid, lo


def _rmsnorm_kernel(x_ref, g_ref, o_ref):
    x = x_ref[...]
    y = x * lax.rsqrt(jnp.mean(x * x, axis=-1, keepdims=True) + RMS_EPS)
    o_ref[...] = (y * g_ref[...]).astype(o_ref.dtype)


def _rmsnorm(x2d, gain, out_dtype, rows=512):
    m, d = x2d.shape
    rows = min(rows, m)
    assert m % rows == 0, (m, rows)
    return pl.pallas_call(
        _rmsnorm_kernel,
        grid=(m // rows,),
        in_specs=[pl.BlockSpec((rows, d), lambda i: (i, 0)), pl.BlockSpec((1, d), lambda i: (0, 0))],
        out_specs=pl.BlockSpec((rows, d), lambda i: (i, 0)),
        out_shape=jax.ShapeDtypeStruct((m, d), out_dtype),
        compiler_params=_params("parallel"),
        name="rmsnorm",
    )(x2d, gain.reshape(1, d).astype(F32))


def _matmul_kernel(*refs, n_lhs, has_res, silu_out):
    o_ref = refs[-1]
    acc = None
    for i in range(n_lhs):
        part = jnp.dot(refs[i][...], refs[n_lhs + i][...], preferred_element_type=F32)
        acc = part if acc is None else acc + part
    if has_res:
        acc = acc + refs[2 * n_lhs][...]
    if silu_out:
        acc = _silu(acc)
    o_ref[...] = acc.astype(o_ref.dtype)


def _matmul(lhs_list, w_list, out_dtype, *, n=None, wcol=None, res=None, silu_out=False, tm=1024, tn=1024,
            name="matmul"):
    m = lhs_list[0].shape[0]
    n = w_list[0][0].shape[1] if n is None else n
    wcol = (lambda j: j) if wcol is None else wcol
    tm, tn = min(tm, m), min(tn, n)
    assert m % tm == 0 and n % tn == 0, (m, n, tm, tn)
    in_specs = [pl.BlockSpec((tm, a.shape[1]), lambda i, j: (i, 0)) for a in lhs_list]
    in_specs += [pl.BlockSpec((a.shape[1], tn), lambda i, j, kblk=kblk: (kblk, wcol(j)))
                 for a, (_, kblk) in zip(lhs_list, w_list)]
    args = list(lhs_list) + [w for w, _ in w_list]
    if res is not None:
        in_specs.append(pl.BlockSpec((tm, tn), lambda i, j: (i, j)))
        args.append(res)
    return pl.pallas_call(
        functools.partial(_matmul_kernel, n_lhs=len(lhs_list), has_res=res is not None, silu_out=silu_out),
        grid=(m // tm, n // tn),
        in_specs=in_specs,
        out_specs=pl.BlockSpec((tm, tn), lambda i, j: (i, j)),
        out_shape=jax.ShapeDtypeStruct((m, n), out_dtype),
        compiler_params=_params("parallel", "arbitrary"),
        name=name,
    )(*args)


CONV_PAD = SUBLANES
CONV_STRIP = LANES


def _horner_dwconv(xs, w_ref, cols):
    acc = w_ref[0:1, cols] * xs
    for k in range(1, w_ref.shape[0]):
        acc = w_ref[k:k + 1, cols] * xs + pltpu.roll(acc, 1, 0)
    return acc


def _band_ok(i, j, var, bq, n_back):
    qc = i >> 6
    kc = j >> 6
    kpos = var * bq - n_back * CHUNK + j
    return (kc >= qc) & (kc <= qc + n_back) & (kpos >= 0)


def _bias_a_kernel(tab_ref, o_ref, *, bq, nk, n_back):
    var = pl.program_id(1)
    for c in range(nk // LANES):
        i = lax.broadcasted_iota(jnp.int32, (bq, LANES), 0)
        j = lax.broadcasted_iota(jnp.int32, (bq, LANES), 1) + c * LANES
        idx = jnp.clip(i - j + n_back * CHUNK, -REL_CLIP, REL_CLIP) + REL_CLIP
        lane = idx & (LANES - 1)
        seg = idx >> 7
        val = jnp.zeros((bq, LANES), F32)
        for s in range(tab_ref.shape[1]):
            tab = jnp.broadcast_to(tab_ref[0, s:s + 1, :], (bq, LANES))
            val = jnp.where(seg == s, jnp.take_along_axis(tab, lane, axis=1), val)
        o_ref[0, 0, :, c * LANES:(c + 1) * LANES] = jnp.where(_band_ok(i, j, var, bq, n_back), val * LOG2E, NEG_INF)


def _bias_b_kernel(slope_ref, o_ref, *, bq, nk, n_back):
    h = pl.program_id(0)
    var = pl.program_id(1)
    i = lax.broadcasted_iota(jnp.int32, (bq, nk), 0)
    j = lax.broadcasted_iota(jnp.int32, (bq, nk), 1)
    dist = jnp.abs(i - j + n_back * CHUNK).astype(F32)
    val = (-slope_ref[h]) * dist
    o_ref[0, 0] = jnp.where(_band_ok(i, j, var, bq, n_back), val * LOG2E, NEG_INF)


def _tile_geometry(n_back, bq):
    back_rows = n_back * CHUNK
    return back_rows + bq, -(-back_rows // bq) + 1


def _window_blocks(n_back, rows):
    back_rows = n_back * CHUNK
    kb = min(rows, back_rows)
    assert rows % kb == 0 and back_rows % kb == 0
    return kb, tuple(range(-(back_rows // kb), rows // kb))


def _bias_tiles_a(rel_table, bq):
    heads, width = rel_table.shape
    nseg = -(-width // LANES)
    tab = jnp.pad(rel_table.astype(F32), ((0, 0), (0, nseg * LANES - width))).reshape(heads, nseg, LANES)
    nk, nvar = _tile_geometry(A_BACK_CHUNKS, bq)
    return pl.pallas_call(
        functools.partial(_bias_a_kernel, bq=bq, nk=nk, n_back=A_BACK_CHUNKS),
        grid=(heads, nvar),
        in_specs=[pl.BlockSpec((1, nseg, LANES), lambda h, v: (h, 0, 0))],
        out_specs=pl.BlockSpec((1, 1, bq, nk), lambda h, v: (h, v, 0, 0)),
        out_shape=jax.ShapeDtypeStruct((heads, nvar, bq, nk), F32),
        compiler_params=_params("parallel", "parallel"),
        name="bias_a",
    )(tab)


def _bias_tiles_b(heads, bq):
    slopes = 2.0 ** (-8.0 * jnp.arange(1, heads + 1, dtype=F32) / heads)
    nk, nvar = _tile_geometry(B_BACK_CHUNKS, bq)
    return pl.pallas_call(
        functools.partial(_bias_b_kernel, bq=bq, nk=nk, n_back=B_BACK_CHUNKS),
        grid=(heads, nvar),
        in_specs=[pl.BlockSpec(memory_space=pltpu.SMEM)],
        out_specs=pl.BlockSpec((1, 1, bq, nk), lambda h, v: (h, v, 0, 0)),
        out_shape=jax.ShapeDtypeStruct((heads, nvar, bq, nk), F32),
        compiler_params=_params("parallel", "parallel"),
        name="bias_b",
    )(slopes)


def _softmax_pv(q, keys, vals, biases, sink, gate):
    logits = [jnp.einsum("nqd,nkd->nqk", q, k, preferred_element_type=F32) * (ATTN_SCALE * LOG2E) + b
              for k, b in zip(keys, biases)]
    m = functools.reduce(jnp.maximum, [jnp.max(s, axis=-1, keepdims=True) for s in logits])
    if sink is not None:
        m = jnp.maximum(m, sink)
    probs = [jnp.exp2(s - m) for s in logits]
    denom = functools.reduce(jnp.add, [jnp.sum(p, axis=-1, keepdims=True) for p in probs])
    if sink is not None:
        denom = denom + jnp.exp2(sink - m)
    o = functools.reduce(jnp.add, [jnp.einsum("nqk,nkd->nqd", p.astype(BF16), v, preferred_element_type=F32)
                                   for p, v in zip(probs, vals)])
    return (o / denom) * gate


def _head_cols(i):
    return slice(i * HEAD_DIM, (i + 1) * HEAD_DIM)


def _attn_prompt_kernel(*refs, nblk, bq, nk, ntile, hps, kvps, hpu, has_sink):
    sink_ref, q_ref = refs[0], refs[1]
    k_refs = refs[2:2 + nblk]
    v_refs = refs[2 + nblk:2 + 2 * nblk]
    g_ref = refs[2 + 2 * nblk]
    b_refs = refs[3 + 2 * nblk:3 + 2 * nblk + ntile]
    o_ref = refs[-1]
    group = hps // kvps
    head0 = pl.program_id(0) * hps
    units = [(kv, r, kv * group + c, hpu) for kv in range(kvps) for r in range(ntile) for c in range(0, group, hpu)]

    def window(blk_refs, kv):
        return jnp.concatenate([ref[:, _head_cols(kv)] for ref in blk_refs], axis=0).astype(BF16)

    def stacked(ref, r, h0, nh):
        return jnp.concatenate([ref[r * bq:(r + 1) * bq, _head_cols(h0 + g)] for g in range(nh)], axis=0)

    kwin = [window(k_refs, kv) for kv in range(kvps)]
    vwin = [window(v_refs, kv) for kv in range(kvps)]
    q = jnp.stack([stacked(q_ref, r, h0, nh) for _, r, h0, nh in units]).astype(BF16)
    k = jnp.stack([kwin[kv][r * bq:r * bq + nk] for kv, r, _, _ in units])
    v = jnp.stack([vwin[kv][r * bq:r * bq + nk] for kv, r, _, _ in units])
    bias = jnp.stack([b_refs[r][h0:h0 + nh, 0].reshape(nh * bq, nk) for _, r, h0, nh in units])
    gate = jnp.stack([stacked(g_ref, r, h0, nh) for _, r, h0, nh in units])
    sink = None
    if has_sink:
        sink = jnp.stack([jnp.concatenate([jnp.full((bq, 1), sink_ref[head0 + h0 + g], F32)
                                           for g in range(nh)], axis=0) for _, _, h0, nh in units])
    out = _softmax_pv(q, [k], [v], [bias], sink, gate).astype(o_ref.dtype)
    for n, (_, r, h0, nh) in enumerate(units):
        for g in range(nh):
            o_ref[r * bq:(r + 1) * bq, _head_cols(h0 + g)] = out[n, g * bq:(g + 1) * bq, :]


def _attn_prompt(qkv, gates, bias, sinks, *, heads, group, n_back, q_col, k_col, v_col, g_col):
    t = qkv.shape[0]
    rows, bq = (ATTN_ROWS_PER_STEP if group == 1 else 2 * ATTN_ROWS_PER_STEP), bias.shape[2]
    ntile = rows // bq
    hps = ATTN_HEADS_PER_STEP
    assert hps % group == 0 and heads % hps == 0
    kvps = hps // group
    nk, nvar = _tile_geometry(n_back, bq)
    kb, offs = _window_blocks(n_back, rows)
    assert t % rows == 0 and bias.shape == (heads, nvar, bq, nk)
    assert q_col % hps == 0 and g_col % hps == 0 and k_col % kvps == 0 and v_col % kvps == 0
    has_sink = sinks is not None
    sink_arr = sinks.astype(F32) * LOG2E if has_sink else jnp.zeros((heads,), F32)

    def kv_spec(col, off):
        return pl.BlockSpec((kb, kvps * HEAD_DIM),
                            lambda hb, qi: (jnp.maximum(qi * (rows // kb) + off, 0), col // kvps + hb))

    def bias_spec(r):
        return pl.BlockSpec((hps, 1, bq, nk), lambda hb, qi: (hb, jnp.minimum(qi * ntile + r, nvar - 1), 0, 0))

    in_specs = [pl.BlockSpec(memory_space=pltpu.SMEM),
                pl.BlockSpec((rows, hps * HEAD_DIM), lambda hb, qi: (qi, q_col // hps + hb))]
    in_specs += [kv_spec(k_col, off) for off in offs]
    in_specs += [kv_spec(v_col, off) for off in offs]
    in_specs += [pl.BlockSpec((rows, hps * HEAD_DIM), lambda hb, qi: (qi, g_col // hps + hb))]
    in_specs += [bias_spec(r) for r in range(ntile)]
    return pl.pallas_call(
        functools.partial(_attn_prompt_kernel, nblk=len(offs), bq=bq, nk=nk, ntile=ntile, hps=hps, kvps=kvps,
                          hpu=min(group, ATTN_HEADS_PER_UNIT), has_sink=has_sink),
        grid=(heads // hps, t // rows),
        in_specs=in_specs,
        out_specs=pl.BlockSpec((rows, hps * HEAD_DIM), lambda hb, qi: (qi, hb)),
        out_shape=jax.ShapeDtypeStruct((t, heads * HEAD_DIM), BF16),
        compiler_params=_params("parallel", "arbitrary"),
        name="attn_prompt",
    )(sink_arr, qkv, *([qkv] * (2 * len(offs))), gates, *([bias] * ntile))


def _attn_step_kernel(sink_ref, qkv_ref, ck_ref, cv_ref, g_ref, b_ref, o_ref, *,
                      heads, group, cache_len, q_col, k_col, v_col, has_sink):
    t = qkv_ref.shape[0]
    kv_heads = heads // group

    def head_block(ref2d, col):
        return ref2d[:, _head_cols(col)]

    def per_kv(fn):
        return jnp.stack([fn(kv) for kv in range(kv_heads)])

    def grouped(ref2d, col0):
        return per_kv(lambda kv: jnp.concatenate(
            [head_block(ref2d, col0 + kv * group + g) for g in range(group)], axis=0))

    def cached(ref, kv):
        return ref[0, pl.ds(kv, cache_len, stride=kv_heads), :].astype(BF16)

    q = grouped(qkv_ref, q_col).astype(BF16)
    keys = (per_kv(lambda kv: cached(ck_ref, kv)),
            per_kv(lambda kv: head_block(qkv_ref, k_col + kv).astype(BF16)))
    vals = (per_kv(lambda kv: cached(cv_ref, kv)),
            per_kv(lambda kv: head_block(qkv_ref, v_col + kv).astype(BF16)))
    bias = b_ref[:, 0].reshape(kv_heads, group * t, b_ref.shape[3])
    biases = (bias[:, :, 0:cache_len], bias[:, :, cache_len:cache_len + t])
    sink = None
    if has_sink:
        sink = per_kv(lambda kv: jnp.concatenate(
            [jnp.full((t, 1), sink_ref[kv * group + g], F32) for g in range(group)], axis=0))
    out = _softmax_pv(q, keys, vals, biases, sink, grouped(g_ref, 0)).astype(o_ref.dtype)
    for kv in range(kv_heads):
        for g in range(group):
            h = kv * group + g
            o_ref[:, h * HEAD_DIM:(h + 1) * HEAD_DIM] = out[kv, g * t:(g + 1) * t, :]


def _attn_step(qkv, cache_k, cache_v, gates, bias, sinks, *, batch, heads, group, q_col, k_col, v_col, g_blk):
    rows = qkv.shape[0]
    t = rows // batch
    cache_rows = cache_k.shape[1]
    cache_len = cache_rows // (heads // group)
    nvar, bq, nk = bias.shape[1:]
    assert t <= CHUNK and cache_len + t <= nk and t <= bq
    has_sink = sinks is not None
    sink_arr = sinks.astype(F32) * LOG2E if has_sink else jnp.zeros((heads,), F32)
    width = heads * HEAD_DIM
    return pl.pallas_call(
        functools.partial(_attn_step_kernel, heads=heads, group=group, cache_len=cache_len,
                          q_col=q_col, k_col=k_col, v_col=v_col, has_sink=has_sink),
        grid=(batch,),
        in_specs=[pl.BlockSpec(memory_space=pltpu.SMEM),
                  pl.BlockSpec((t, qkv.shape[1]), lambda b: (b, 0)),
                  pl.BlockSpec((1, cache_rows, HEAD_DIM), lambda b: (b, 0, 0)),
                  pl.BlockSpec((1, cache_rows, HEAD_DIM), lambda b: (b, 0, 0)),
                  pl.BlockSpec((t, width), lambda b: (b, g_blk)),
                  pl.BlockSpec((heads, 1, t, nk), lambda b: (0, nvar - 1, 0, 0))],
        out_specs=pl.BlockSpec((t, width), lambda b: (b, 0)),
        out_shape=jax.ShapeDtypeStruct((rows, width), BF16),
        compiler_params=_params("parallel"),
        name="attn_step",
    )(sink_arr, qkv, cache_k, cache_v, gates, bias)


def _ssd_kernel(xbc_ref, zs_ref, dtr_ref, gs_ref, b_ref, c2_ref, h2_ref, conv0_ref, dconv0_ref, h0_ref, cw_ref, cb_ref,
                dw_ref, dtb_ref, alog_ref, dsk_ref, cng_ref, exp_ref,
                y_ref, yd_ref, hout_ref, convout_ref, dconvout_ref, h_s, xpad_s, upad_s, act_s, gated_s, *, rows):
    c = pl.program_id(1)
    q = SSD_Q
    width = y_ref.shape[2]
    gw = width // C_GROUPS
    pad0 = CONV_PAD

    @pl.when(c == 0)
    def _():
        h_s[...] = h0_ref[0]
        upad_s[0:pad0, :] = dconv0_ref[0]
        xpad_s[0:pad0, :] = conv0_ref[0]

    upad_s[pad0:pad0 + rows, :] = c2_ref[0] * h2_ref[0]
    for s0 in range(0, upad_s.shape[1], CONV_STRIP):
        cols = slice(s0, s0 + CONV_STRIP)
        v = _horner_dwconv(upad_s[:, cols], dw_ref, cols)[pad0:pad0 + rows]
        yd_ref[0, :, cols] = (b_ref[0, :, cols] * v * gs_ref[0, :, cols]).astype(yd_ref.dtype)
    dconvout_ref[0] = upad_s[rows:rows + pad0, :]
    upad_s[0:pad0, :] = upad_s[rows:rows + pad0, :]

    xpad_s[pad0:pad0 + rows, :] = xbc_ref[0]
    for s0 in range(0, xpad_s.shape[1], CONV_STRIP):
        cols = slice(s0, s0 + CONV_STRIP)
        conv = _horner_dwconv(xpad_s[:, cols], cw_ref, cols)[pad0:pad0 + rows]
        act_s[0:rows, cols] = _silu(conv + cb_ref[:, cols])
    if rows < q:
        act_s[rows:q, :] = jnp.zeros((q - rows, act_s.shape[1]), F32)
    convout_ref[0] = xpad_s[rows:rows + pad0, :]
    xpad_s[0:pad0, :] = xpad_s[rows:rows + pad0, :]

    def act(r0, cols):
        return act_s[r0:r0 + q, cols]

    row = lax.broadcasted_iota(jnp.int32, (q, LANES), 0)
    lane = lax.broadcasted_iota(jnp.int32, (q, LANES), 1)
    causal = row >= (lane & (C_HEAD_DIM - 1))
    low_half = lane < C_HEAD_DIM
    diag_g = (lax.broadcasted_iota(jnp.int32, (q, gw), 0)
              == (lax.broadcasted_iota(jnp.int32, (q, gw), 1) & (C_HEAD_DIM - 1)))
    tril = (lax.broadcasted_iota(jnp.int32, (q, q), 0) >= lax.broadcasted_iota(jnp.int32, (q, q), 1)).astype(BF16)
    tril3 = jnp.concatenate([tril] * 3, axis=1)
    neg_a = -jnp.exp(alog_ref[...]) * LOG2E
    valid = min(rows, q)

    def pad_rows(v):
        return v if valid == q else jnp.concatenate([v, jnp.zeros((q - valid, v.shape[1]), F32)], axis=0)

    for r0 in range(0, rows, q):
        dt = jnp.where(row < valid, jax.nn.softplus(pad_rows(dtr_ref[0, r0:r0 + valid, :]) + dtb_ref[...]), 0.0)
        cum = jnp.dot(tril3, jnp.concatenate(_split3(dt * neg_a), axis=0), preferred_element_type=F32)
        spread = jnp.concatenate(_split3(jnp.concatenate([cum, dt], axis=0)), axis=1)
        ssq = jnp.zeros((q, 1), F32)
        for g in range(C_GROUPS):
            cols = slice(g * gw, (g + 1) * gw)
            ex = jnp.dot(spread, exp_ref[:, cols], preferred_element_type=F32)
            cexp, dtx = ex[0:q], ex[q:2 * q]
            x_g = act(r0, cols)
            b_g = act(r0, slice(width + g * C_STATE, width + (g + 1) * C_STATE)).astype(BF16)
            c_g = act(r0, slice(width + (C_GROUPS + g) * C_STATE, width + (C_GROUPS + g + 1) * C_STATE)).astype(BF16)
            xdt = x_g * dtx
            cum_row = jnp.sum(jnp.where(diag_g, cexp, 0.0), axis=0, keepdims=True)
            cum_last = cexp[q - 1:q, :]
            scores = lax.dot_general(c_g, jnp.concatenate([b_g, b_g], axis=0), (((1,), (1,)), ((), ())),
                                     preferred_element_type=F32)
            h_g = h_s[:, cols]
            y_off = jnp.dot(c_g, h_g.astype(BF16), preferred_element_type=F32) * jnp.exp2(cexp)
            y_diag = []
            for j in range(gw // LANES):
                pc = slice(j * LANES, (j + 1) * LANES)
                seg = cexp[:, pc] - cum_row[:, pc]
                lmat = jnp.exp2(jnp.where(causal, seg, NEG_INF))
                xp = xdt[:, pc]
                blockdiag = jnp.concatenate([jnp.where(low_half, xp, 0.0), jnp.where(low_half, 0.0, xp)], axis=0)
                y_diag.append(jnp.dot((scores * lmat).astype(BF16), blockdiag.astype(BF16),
                                      preferred_element_type=F32))
            y = jnp.concatenate(y_diag, axis=1) + y_off + dsk_ref[:, cols] * x_g
            xw = (xdt * jnp.exp2(cum_last - cexp)).astype(BF16)
            h_s[:, cols] = h_g * jnp.exp2(cum_last) + lax.dot_general(
                b_g, xw, (((0,), (0,)), ((), ())), preferred_element_type=F32)
            gated = y * pad_rows(zs_ref[0, r0:r0 + valid, cols])
            gated_s[:, cols] = gated
            ssq = ssq + jnp.sum(gated * gated, axis=-1, keepdims=True)
        inv = lax.rsqrt(ssq * (1.0 / width) + RMS_EPS)
        y_ref[0, r0:r0 + valid, :] = ((gated_s[...] * inv)[0:valid] * cng_ref[...]).astype(y_ref.dtype)

    @pl.when(c == pl.num_programs(1) - 1)
    def _():
        hout_ref[0] = h_s[...]


def _ssd(xbc, zs, dtr, gs, bch, states, p, *, rows):
    b, t, conv_dim = xbc.shape
    width = zs.shape[2]
    d_width = gs.shape[2]
    n_heads = width // C_HEAD_DIM
    ssm_state, conv_c_state, conv_d_state = states
    assert t % rows == 0 and rows % 16 == 0 and (rows <= SSD_Q or rows % SSD_Q == 0) and n_heads <= LANES
    assert conv_dim % CONV_STRIP == 0 and d_width % CONV_STRIP == 0
    pad_lanes = lambda v: jnp.pad(v.astype(F32), (0, LANES - v.shape[0])).reshape(1, LANES)
    pad_hist = lambda s: jnp.pad(s.astype(F32), ((0, 0), (CONV_PAD - s.shape[1], 0), (0, 0)))
    h0t = jnp.swapaxes(ssm_state.astype(F32).reshape(b, width, C_STATE), 1, 2)
    expand = (jnp.arange(LANES)[:, None] == (jnp.arange(width)[None, :] // C_HEAD_DIM)).astype(BF16)
    expand3 = jnp.concatenate([expand] * 3, axis=0)
    full = lambda shape: pl.BlockSpec(shape, lambda bi, ci: (0,) * len(shape))
    per_b = lambda shape: pl.BlockSpec(shape, lambda bi, ci: (bi,) + (0,) * (len(shape) - 1))
    blk = lambda w, k=0: pl.BlockSpec((1, rows, w), lambda bi, ci: (bi, ci, k))
    y, yd, h_out, conv_out, dconv_out = pl.pallas_call(
        functools.partial(_ssd_kernel, rows=rows),
        grid=(b, t // rows),
        in_specs=[blk(conv_dim), blk(width), blk(LANES),
                  blk(d_width), blk(d_width, 0), blk(d_width, 1), blk(d_width, 2),
                  per_b((1, CONV_PAD, conv_dim)), per_b((1, CONV_PAD, d_width)), per_b((1, C_STATE, width)),
                  full((C_CONV_W, conv_dim)), full((1, conv_dim)), full((D_CONV_W, d_width)),
                  full((1, LANES)), full((1, LANES)), full((1, width)), full((1, width)),
                  full((3 * LANES, width))],
        out_specs=[blk(width), blk(d_width), per_b((1, C_STATE, width)),
                   per_b((1, CONV_PAD, conv_dim)), per_b((1, CONV_PAD, d_width))],
        out_shape=[jax.ShapeDtypeStruct((b, t, width), BF16),
                   jax.ShapeDtypeStruct((b, t, d_width), BF16),
                   jax.ShapeDtypeStruct((b, C_STATE, width), F32),
                   jax.ShapeDtypeStruct((b, CONV_PAD, conv_dim), F32),
                   jax.ShapeDtypeStruct((b, CONV_PAD, d_width), F32)],
        scratch_shapes=[pltpu.VMEM((C_STATE, width), F32),
                        pltpu.VMEM((CONV_PAD + rows, conv_dim), F32),
                        pltpu.VMEM((CONV_PAD + rows, d_width), F32),
                        pltpu.VMEM((max(rows, SSD_Q), conv_dim), F32),
                        pltpu.VMEM((SSD_Q, width), F32)],
        compiler_params=_params("parallel", "arbitrary"),
        name="ssd",
    )(xbc, zs, dtr, gs, bch, bch, bch, pad_hist(conv_c_state), pad_hist(conv_d_state), h0t,
      p["conv_c_w"].astype(F32), p["conv_c_b"].astype(F32).reshape(1, conv_dim), p["conv_d_w"].astype(F32),
      pad_lanes(p["dt_bias"]), pad_lanes(p["a_log"]),
      jnp.repeat(p["d_skip"].astype(F32), C_HEAD_DIM).reshape(1, width),
      p["c_norm"].astype(F32).reshape(1, width), expand3)
    new_state = jnp.swapaxes(h_out, 1, 2).reshape(b, n_heads, C_HEAD_DIM, C_STATE)
    return (y, yd, new_state, conv_out[:, CONV_PAD - (C_CONV_W - 1):, :],
            dconv_out[:, CONV_PAD - (D_CONV_W - 1):, :])


TN_WIDE = 1024
TN_NARROW = 512
TM_TALL = 2048


def _even_weights(w_in, w_out):
    aw = w_out.shape[0] // 2
    bkv = (w_in.shape[1] - 6 * aw) // 2
    assert aw % TN_WIDE == 0 and (2 * bkv) % TN_NARROW == 0 and aw % TN_NARROW == 0
    return dict(w_in=w_in.astype(BF16), w_out=w_out.astype(BF16), aw=aw, bkv=bkv)


def _in_proj_even(hn, w, kv_dtype):
    aw, wb = w["aw"], w["w_in"]
    qkv_a = _matmul([hn], [(wb, 0)], kv_dtype, n=3 * aw, tn=TN_WIDE, name="in_proj_qkv_a")
    nq = aw // TN_NARROW
    qkv_b = _matmul([hn], [(wb, 0)], kv_dtype, n=aw + 2 * w["bkv"], tm=TM_TALL, tn=TN_NARROW, name="in_proj_qkv_b",
                    wcol=lambda j: jnp.where(j < nq, 4 * nq + j, 6 * nq + j - nq))
    ng = aw // TN_WIDE
    gates = _matmul([hn], [(wb, 0)], F32, n=2 * aw, tn=TN_WIDE, name="in_proj_gates", silu_out=True,
                    wcol=lambda j: jnp.where(j < ng, 3 * ng + j, 5 * ng + j - ng))
    return qkv_a, qkv_b, gates


def _attn_layer(x2d, hn, w, bias_a, bias_b, sinks, caches, batch):
    aw, bkv = w["aw"], w["bkv"]
    heads = aw // HEAD_DIM
    qkv_a, qkv_b, gates = _in_proj_even(hn, w, BF16 if caches is None else F32)
    kvh_b = bkv // HEAD_DIM
    if caches is None:
        oa = _attn_prompt(qkv_a, gates, bias_a, None, heads=heads, group=1, n_back=A_BACK_CHUNKS,
                          q_col=0, k_col=heads, v_col=2 * heads, g_col=0)
        ob = _attn_prompt(qkv_b, gates, bias_b, sinks, heads=heads, group=heads // kvh_b, n_back=B_BACK_CHUNKS,
                          q_col=0, k_col=heads, v_col=heads + kvh_b, g_col=heads)
    else:
        ck_a, cv_a, ck_b, cv_b = caches
        flat = lambda c: c.astype(F32).reshape(c.shape[0], c.shape[1] * c.shape[2], c.shape[3])
        oa = _attn_step(qkv_a, flat(ck_a), flat(cv_a), gates, bias_a, None, batch=batch, heads=heads, group=1,
                        q_col=0, k_col=heads, v_col=2 * heads, g_blk=0)
        ob = _attn_step(qkv_b, flat(ck_b), flat(cv_b), gates, bias_b, sinks, batch=batch, heads=heads,
                        group=heads // kvh_b, q_col=0, k_col=heads, v_col=heads + kvh_b, g_blk=1)
    x_new = _matmul([oa, ob], [(w["w_out"], 0), (w["w_out"], 1)], F32, res=x2d, tn=TN_WIDE, name="out_proj_even")
    return x_new, qkv_a, qkv_b


def _odd_weights(w_in, w_out, c_width, conv_dim, c_heads):
    o1 = c_width + conv_dim
    assert c_width % TN_WIDE == 0 and conv_dim % TN_WIDE == 0
    wb = w_in.astype(BF16)
    return dict(
        w_in=wb,
        dt=jnp.pad(wb[:, o1:o1 + c_heads], ((0, 0), (0, LANES - c_heads))),
        gbch=wb[:, o1 + c_heads:],
        w_out=w_out.astype(BF16))


def _ssm_conv_layer(x2d, hn, w, p, states, batch, rows):
    c_width = p["c_norm"].shape[0]
    conv_dim = p["conv_c_w"].shape[1]
    d_width = p["conv_d_w"].shape[1]
    t = x2d.shape[0] // batch
    nz = c_width // TN_WIDE
    assert c_width % d_width == 0 and d_width % TN_WIDE == 0
    zs = _matmul([hn], [(w["w_in"], 0)], F32, n=c_width, tn=TN_WIDE, silu_out=True,
                 name="in_proj_z").reshape(batch, t, c_width)
    xbc = _matmul([hn], [(w["w_in"], 0)], F32, n=conv_dim, tn=TN_WIDE, wcol=lambda j: nz + j,
                  name="in_proj_xbc").reshape(batch, t, conv_dim)
    dtr = _matmul([hn], [(w["dt"], 0)], F32, name="in_proj_dt").reshape(batch, t, LANES)
    ng = d_width // TN_WIDE
    gs = _matmul([hn], [(w["gbch"], 0)], F32, n=d_width, tn=TN_WIDE, silu_out=True,
                 name="in_proj_gate_d").reshape(batch, t, d_width)
    bch = _matmul([hn], [(w["gbch"], 0)], F32, n=3 * d_width, tn=TN_WIDE, wcol=lambda j: ng + j,
                  name="in_proj_bch").reshape(batch, t, 3 * d_width)
    yc, yd, new_ssm, new_cc, new_dc = _ssd(xbc, zs, dtr, gs, bch, states, p, rows=rows)
    x_new = _matmul([yc.reshape(batch * t, c_width), yd.reshape(batch * t, d_width)],
                    [(w["w_out"], 0), (w["w_out"], c_width // d_width)], F32,
                    res=x2d, tn=TN_NARROW, name="out_proj_odd")
    return x_new, new_ssm, new_cc, new_dc


def kernel(x_prompt, x_sample, cache_a_k, cache_a_v, cache_b_k, cache_b_v, state_c_ssm, state_c_conv, state_d_conv, norm_even, w_in_even, w_out_even, rel_bias_a, sinks_b, norm_odd, w_in_odd, w_out_odd, conv_c_w, conv_c_b, dt_bias, a_log, d_skip, c_norm, conv_d_w, final_norm):
    bp, tp, d = x_prompt.shape
    bs, ts, _ = x_sample.shape
    assert bp == 1, "the prompt attention kernel handles one prompt sequence"
    dt_out = x_prompt.dtype
    depth = norm_even.shape[0] + norm_odd.shape[0]
    xp = x_prompt.astype(F32).reshape(bp * tp, d)
    xs = x_sample.astype(F32).reshape(bs * ts, d)
    keep_a = min(A_BACK_CHUNKS * CHUNK, tp)
    keep_b = min(B_BACK_CHUNKS * CHUNK, tp)
    outs = {k: [] for k in ("p_ak", "p_av", "p_bk", "p_bv", "p_ssm", "p_cc", "p_dc",
                            "s_ak", "s_av", "s_bk", "s_bv", "s_ssm", "s_cc", "s_dc")}
    for layer in range(depth):
        i = layer // 2
        if layer % 2 == 0:
            w = _even_weights(w_in_even[i], w_out_even[i])
            aw, bkv = w["aw"], w["bkv"]
            heads = aw // HEAD_DIM
            bias_a = _bias_tiles_a(rel_bias_a[i], ATTN_TILE_ROWS)
            bias_b = _bias_tiles_b(heads, ATTN_TILE_ROWS)
            hp = _rmsnorm(xp, norm_even[i], BF16)
            hs = _rmsnorm(xs, norm_even[i], BF16)
            xp_new, _, _ = _attn_layer(xp, hp, w, bias_a, bias_b, sinks_b[i], None, bp)
            kv_a = _matmul([hp[tp - keep_a:]], [(w["w_in"], 0)], F32, n=2 * aw, tn=TN_WIDE,
                           wcol=lambda j: aw // TN_WIDE + j, name="cache_kv_a")
            kv_b = _matmul([hp[tp - keep_b:]], [(w["w_in"], 0)], F32, n=2 * bkv, tn=TN_NARROW,
                           wcol=lambda j: 6 * aw // TN_NARROW + j, name="cache_kv_b")
            outs["p_ak"].append(kv_a[:, :aw].reshape(bp, keep_a, heads, HEAD_DIM))
            outs["p_av"].append(kv_a[:, aw:].reshape(bp, keep_a, heads, HEAD_DIM))
            outs["p_bk"].append(kv_b[:, :bkv].reshape(bp, keep_b, bkv // HEAD_DIM, HEAD_DIM))
            outs["p_bv"].append(kv_b[:, bkv:].reshape(bp, keep_b, bkv // HEAD_DIM, HEAD_DIM))
            xs_new, qkv_a, qkv_b = _attn_layer(xs, hs, w, bias_a, bias_b, sinks_b[i],
                                               (cache_a_k[i], cache_a_v[i], cache_b_k[i], cache_b_v[i]), bs)
            outs["s_ak"].append(qkv_a[:, aw:2 * aw].reshape(bs, ts, heads, HEAD_DIM))
            outs["s_av"].append(qkv_a[:, 2 * aw:].reshape(bs, ts, heads, HEAD_DIM))
            outs["s_bk"].append(qkv_b[:, aw:aw + bkv].reshape(bs, ts, bkv // HEAD_DIM, HEAD_DIM))
            outs["s_bv"].append(qkv_b[:, aw + bkv:].reshape(bs, ts, bkv // HEAD_DIM, HEAD_DIM))
            xp, xs = xp_new, xs_new
        else:
            c_width = c_norm.shape[1]
            conv_dim = conv_c_w.shape[2]
            c_heads = dt_bias.shape[1]
            d_width = conv_d_w.shape[2]
            w = _odd_weights(w_in_odd[i], w_out_odd[i], c_width, conv_dim, c_heads)
            p = dict(conv_c_w=conv_c_w[i], conv_c_b=conv_c_b[i], dt_bias=dt_bias[i], a_log=a_log[i],
                     d_skip=d_skip[i], c_norm=c_norm[i], conv_d_w=conv_d_w[i])
            zero = (jnp.zeros((bp, c_heads, C_HEAD_DIM, C_STATE), F32),
                    jnp.zeros((bp, C_CONV_W - 1, conv_dim), F32),
                    jnp.zeros((bp, D_CONV_W - 1, d_width), F32))
            hp = _rmsnorm(xp, norm_odd[i], BF16)
            hs = _rmsnorm(xs, norm_odd[i], BF16)
            xp, ssm, cc, dc = _ssm_conv_layer(xp, hp, w, p, zero, bp, min(SSD_ROWS_PER_STEP, tp))
            xs, sssm, scc, sdc = _ssm_conv_layer(xs, hs, w, p, (state_c_ssm[i], state_c_conv[i], state_d_conv[i]),
                                                 bs, ts)
            outs["p_ssm"].append(ssm.astype(dt_out)); outs["p_cc"].append(cc.astype(dt_out))
            outs["p_dc"].append(dc.astype(dt_out))
            outs["s_ssm"].append(sssm.astype(dt_out)); outs["s_cc"].append(scc.astype(dt_out))
            outs["s_dc"].append(sdc.astype(dt_out))
    y_prompt = _rmsnorm(xp, final_norm, dt_out).reshape(bp, tp, d)
    y_sample = _rmsnorm(xs, final_norm, dt_out).reshape(bs, ts, d)
    st = lambda k: jnp.stack(outs[k])
    return (y_prompt, y_sample, st("p_ak"), st("p_av"), st("p_bk"), st("p_bv"), st("p_ssm"), st("p_cc"), st("p_dc"),
            st("s_ak"), st("s_av"), st("s_bk"), st("s_bv"), st("s_ssm"), st("s_cc"), st("s_dc"))
```
